```python
import math
import jax
import jax.numpy as jnp
from jax import lax
import numpy as np

D_MODEL = 2048
BATCH = 4
SEQ = 8192
DEPTH = 1

A_HEADS = 4
A_QK_DIM = 256
A_V_DIM = 512
A_CONV = 4
CHUNK = 128
B_HEADS = 8
B_HEAD_DIM = 128
Q_BLOCK = 128
D_FF = 5632
FFN_CONV = 3
EPS = 1e-6

A_QK = A_HEADS * A_QK_DIM
A_V = A_HEADS * A_V_DIM
B_QK = B_HEADS * 2 * B_HEAD_DIM
B_V = B_HEADS * 2 * B_HEAD_DIM
SPLIT_SIZES = (2 * A_QK, A_V, A_V, 2 * A_HEADS, B_QK, B_QK, B_V, D_MODEL, D_MODEL)
P_IN = sum(SPLIT_SIZES)

kernel_name = 'hybrid_mlstm_diffattn_convffn'


def rmsnorm(x, g):
    xf = x.astype(jnp.float32)
    y = xf * lax.rsqrt(jnp.mean(xf * xf, axis=-1, keepdims=True) + EPS)
    return (y * g.astype(jnp.float32)).astype(x.dtype)


def causal_dwconv(x, w, b):
    k_w = w.shape[0]
    s = x.shape[1]
    xp = jnp.pad(x, ((0, 0), (k_w - 1, 0), (0, 0)))
    y = b + xp[:, 0:s, :] * w[0]
    for j in range(1, k_w):
        y = y + xp[:, j:j + s, :] * w[j]
    return y


def alibi_slopes(n):
    return jnp.asarray(2.0 ** (-8.0 * np.arange(1, n + 1) / n), dtype=jnp.float32)


def mlstm_chunkwise(q, k, v, i_pre, f_pre):
    bsz, nh, s, dk = q.shape
    dv = v.shape[-1]
    nc = s // CHUNK

    def to_chunks(a):
        a = a.reshape((bsz, nh, nc, CHUNK) + a.shape[3:])
        return jnp.moveaxis(a, 2, 0)

    logf = jax.nn.log_sigmoid(f_pre)
    logi = i_pre
    tril = jnp.tril(jnp.ones((CHUNK, CHUNK), dtype=bool))

    def step(carry, inp):
        c_st, n_st, m_st = carry
        qc, kc, vc, li, lf = inp
        b = jnp.cumsum(lf, axis=-1)
        dmat = jnp.where(tril, b[..., :, None] - b[..., None, :] + li[..., None, :], -jnp.inf)
        inter = b + m_st[..., None]
        m_t = jnp.maximum(jnp.max(dmat, axis=-1), inter)
        w_in = jnp.exp(dmat - m_t[..., None])
        g_inter = jnp.exp(inter - m_t)
        p = jnp.einsum('bhtd,bhsd->bhts', qc, kc) * w_in
        num = jnp.einsum('bhts,bhsv->bhtv', p, vc) + g_inter[..., None] * jnp.einsum('bhtd,bhdv->bhtv', qc, c_st)
        den = jnp.sum(p, axis=-1) + g_inter * jnp.einsum('bhtd,bhd->bht', qc, n_st)
        h = num / jnp.maximum(jnp.abs(den), jnp.exp(-m_t))[..., None]
        b_last = b[..., -1]
        ws = b_last[..., None] - b + li
        m_new = jnp.maximum(b_last + m_st, jnp.max(ws, axis=-1))
        decay = jnp.exp(b_last + m_st - m_new)
        ws = jnp.exp(ws - m_new[..., None])
        c_new = decay[..., None, None] * c_st + jnp.einsum('bhs,bhsd,bhsv->bhdv', ws, kc, vc)
        n_new = decay[..., None] * n_st + jnp.einsum('bhs,bhsd->bhd', ws, kc)
        return (c_new, n_new, m_new), h

    init = (jnp.zeros((bsz, nh, dk, dv), jnp.float32),
            jnp.zeros((bsz, nh, dk), jnp.float32),
            jnp.zeros((bsz, nh), jnp.float32))
    _, h = lax.scan(step, init, (to_chunks(q), to_chunks(k), to_chunks(v), to_chunks(logi), to_chunks(logf)))
    h = jnp.moveaxis(h, 0, 2)
    return h.reshape(bsz, nh, s, dv)


def diff_attention(q, k, v, lam, lam_init, subln_g):
    bsz, nh, s = q.shape[:3]
    nb = s // Q_BLOCK
    slopes = alibi_slopes(nh)[:, None, None, None]
    pos = jnp.arange(s)
    scale = B_HEAD_DIM ** -0.5
    qb = jnp.moveaxis(q.reshape(bsz, nh, nb, Q_BLOCK, 2, B_HEAD_DIM), 2, 0)

    def block(args):
        qblk, i = args
        t = i * Q_BLOCK + jnp.arange(Q_BLOCK)
        dist = t[:, None] - pos[None, :]
        logits = jnp.einsum('bhqcd,bhkcd->bhcqk', qblk, k) * scale - slopes * dist.astype(jnp.float32)
        logits = jnp.where(dist >= 0, logits, -jnp.inf)
        p = jax.nn.softmax(logits, axis=-1)
        a = p[:, :, 0] - lam * p[:, :, 1]
        return jnp.einsum('bhqk,bhkv->bhqv', a, v)

    o = lax.map(block, (qb, jnp.arange(nb)))
    o = rmsnorm(o, subln_g) * (1.0 - lam_init)
    o = jnp.transpose(o, (1, 0, 3, 2, 4))
    return o.reshape(bsz, s, nh * 2 * B_HEAD_DIM)


def setup_inputs(seed: int = 0) -> dict:
    key = jax.random.key(seed)
    ks = jax.random.split(key, 20)
    f32 = jnp.float32
    nrm = lambda kk, shape: jax.random.normal(kk, shape, dtype=f32)
    x = nrm(ks[0], (BATCH, SEQ, D_MODEL))
    norm1_g = 1.0 + 0.02 * nrm(ks[1], (DEPTH, D_MODEL))
    w_in = nrm(ks[2], (DEPTH, D_MODEL, P_IN)) * D_MODEL ** -0.5
    i_bias = 0.1 * nrm(ks[3], (DEPTH, A_HEADS))
    f_bias = jnp.linspace(3.0, 6.0, A_HEADS, dtype=f32)[None, :] + 0.1 * nrm(ks[4], (DEPTH, A_HEADS))
    if_bias = jnp.concatenate([i_bias, f_bias], axis=-1)
    qk_conv_w = nrm(ks[5], (DEPTH, A_CONV, 2 * A_QK)) * A_CONV ** -0.5
    qk_conv_b = 0.02 * nrm(ks[6], (DEPTH, 2 * A_QK))
    mlstm_norm_g = 1.0 + 0.02 * nrm(ks[7], (DEPTH, A_V))
    q_norm_g = 1.0 + 0.02 * nrm(ks[8], (DEPTH, B_HEAD_DIM))
    k_norm_g = 1.0 + 0.02 * nrm(ks[9], (DEPTH, B_HEAD_DIM))
    diff_lambda = 0.1 * nrm(ks[10], (DEPTH, 4, B_HEAD_DIM))
    subln_g = 1.0 + 0.02 * nrm(ks[11], (DEPTH, 2 * B_HEAD_DIM))
    w_out = nrm(ks[12], (DEPTH, D_MODEL, D_MODEL)) * D_MODEL ** -0.5
    norm2_g = 1.0 + 0.02 * nrm(ks[13], (DEPTH, D_MODEL))
    w_up = nrm(ks[14], (DEPTH, D_MODEL, 2 * D_FF)) * D_MODEL ** -0.5
    ffn_conv_w = nrm(ks[15], (DEPTH, FFN_CONV, 2 * D_FF)) * FFN_CONV ** -0.5
    ffn_conv_b = 0.02 * nrm(ks[16], (DEPTH, 2 * D_FF))
    w_down = nrm(ks[17], (DEPTH, D_FF, D_MODEL)) * D_FF ** -0.5
    return {'x': x, 'norm1_g': norm1_g, 'w_in': w_in, 'if_bias': if_bias,
            'qk_conv_w': qk_conv_w, 'qk_conv_b': qk_conv_b, 'mlstm_norm_g': mlstm_norm_g,
            'q_norm_g': q_norm_g, 'k_norm_g': k_norm_g, 'diff_lambda': diff_lambda,
            'subln_g': subln_g, 'w_out': w_out, 'norm2_g': norm2_g, 'w_up': w_up,
            'ffn_conv_w': ffn_conv_w, 'ffn_conv_b': ffn_conv_b, 'w_down': w_down}


def reference(x, norm1_g, w_in, if_bias, qk_conv_w, qk_conv_b, mlstm_norm_g, q_norm_g, k_norm_g,
              diff_lambda, subln_g, w_out, norm2_g, w_up, ffn_conv_w, ffn_conv_b, w_down):
    f32 = jnp.float32
    bsz, s, _ = x.shape
    split_idx = np.cumsum(SPLIT_SIZES)[:-1].tolist()
    for l in range(DEPTH):
        h = rmsnorm(x, norm1_g[l])
        proj = h @ w_in[l]
        a_qk, a_v, a_o, a_if, b_q, b_k, b_v, g_a, g_b = jnp.split(proj, split_idx, axis=-1)

        qk = jax.nn.silu(causal_dwconv(a_qk, qk_conv_w[l], qk_conv_b[l]))
        a_q, a_k = jnp.split(qk, 2, axis=-1)
        q_m = a_q.reshape(bsz, s, A_HEADS, A_QK_DIM).transpose(0, 2, 1, 3).astype(f32) * (A_QK_DIM ** -0.5)
        k_m = a_k.reshape(bsz, s, A_HEADS, A_QK_DIM).transpose(0, 2, 1, 3).astype(f32)
        v_m = a_v.reshape(bsz, s, A_HEADS, A_V_DIM).transpose(0, 2, 1, 3).astype(f32)
        gates = (a_if + if_bias[l]).astype(f32).reshape(bsz, s, 2, A_HEADS)
        i_pre = gates[:, :, 0].transpose(0, 2, 1)
        f_pre = gates[:, :, 1].transpose(0, 2, 1)
        h_m = mlstm_chunkwise(q_m, k_m, v_m, i_pre, f_pre)
        h_m = rmsnorm(h_m, mlstm_norm_g[l].reshape(A_HEADS, 1, A_V_DIM))
        h_m = h_m.transpose(0, 2, 1, 3).reshape(bsz, s, A_V)
        y_a = jax.nn.sigmoid(a_o) * h_m.astype(x.dtype)

        q_d = rmsnorm(b_q.reshape(bsz, s, B_HEADS, 2, B_HEAD_DIM).astype(f32), q_norm_g[l]).transpose(0, 2, 1, 3, 4)
        k_d = rmsnorm(b_k.reshape(bsz, s, B_HEADS, 2, B_HEAD_DIM).astype(f32), k_norm_g[l]).transpose(0, 2, 1, 3, 4)
        v_d = b_v.reshape(bsz, s, B_HEADS, 2 * B_HEAD_DIM).transpose(0, 2, 1, 3).astype(f32)
        lam_init = 0.8 - 0.6 * math.exp(-0.3 * l)
        lp = diff_lambda[l].astype(f32)
        lam = jnp.exp(jnp.sum(lp[0] * lp[1])) - jnp.exp(jnp.sum(lp[2] * lp[3])) + lam_init
        y_b = diff_attention(q_d, k_d, v_d, lam, lam_init, subln_g[l]).astype(x.dtype)

        y = jax.nn.sigmoid(g_a) * y_a + jax.nn.sigmoid(g_b) * y_b
        x = x + y @ w_out[l]

        h = rmsnorm(x, norm2_g[l])
        u = causal_dwconv(h @ w_up[l], ffn_conv_w[l], ffn_conv_b[l])
        u_g, u_v = jnp.split(u, 2, axis=-1)
        x = x + (jax.nn.silu(u_g) * u_v) @ w_down[l]
    return x
```

```python
import functools
import math

import numpy as np
import jax
import jax.numpy as jnp
from jax import lax
from jax.experimental import pallas as pl
from jax.experimental.pallas import tpu as pltpu

F32 = jnp.float32
BF16 = jnp.bfloat16

D_MODEL = 2048
A_HEADS = 4
A_QK_DIM = 256
A_V_DIM = 512
A_CONV = 4
CHUNK = 128
B_HEADS = 8
B_HEAD_DIM = 128
D_FF = 5632
FFN_CONV = 3
EPS = 1e-6

A_QK = A_HEADS * A_QK_DIM
A_V = A_HEADS * A_V_DIM
B_QK = B_HEADS * 2 * B_HEAD_DIM
B_V = B_HEADS * 2 * B_HEAD_DIM
SPLIT_SIZES = (2 * A_QK, A_V, A_V, 2 * A_HEADS, B_QK, B_QK, B_V, D_MODEL, D_MODEL)

SUBLANES = 8
HALO = SUBLANES
VMEM_LIMIT = 48 * 1024 * 1024


def _params(*sem):
    return pltpu.CompilerParams(dimension_semantics=sem, vmem_limit_bytes=VMEM_LIMIT)


def _dot(a, b):
    return jnp.dot(a, b, preferred_element_type=F32)


def _dot_nt(a, b):
    return lax.dot_general(a, b, (((1,), (1,)), ((), ())), preferred_element_type=F32)


def _sigmoid(x):
    return 1.0 / (1.0 + jnp.exp(-x))


def _rmsnorm_kernel(x_ref, g_ref, o_ref):
    x = x_ref[...]
    ms = jnp.mean(x * x, axis=-1, keepdims=True)
    o_ref[...] = (x * lax.rsqrt(ms + EPS) * g_ref[...]).astype(o_ref.dtype)


def _rmsnorm(x, g, tm):
    n, d = x.shape
    return pl.pallas_call(
        _rmsnorm_kernel,
        grid=(n // tm,),
        in_specs=[pl.BlockSpec((tm, d), lambda i: (i, 0)),
                  pl.BlockSpec((1, d), lambda i: (0, 0))],
        out_specs=pl.BlockSpec((tm, d), lambda i: (i, 0)),
        out_shape=jax.ShapeDtypeStruct((n, d), BF16),
        compiler_params=_params("parallel"),
        name="rmsnorm1",
    )(x, g.reshape(1, d))


def _proj_plain_kernel(h_ref, w_ref, o_ref):
    o_ref[...] = _dot(h_ref[...], w_ref[...]).astype(o_ref.dtype)


def _proj_sigmoid_kernel(h_ref, w_ref, o_ref):
    o_ref[...] = _sigmoid(_dot(h_ref[...], w_ref[...])).astype(o_ref.dtype)


def _proj_headnorm_kernel(h_ref, w_ref, g_ref, o_ref, *, group):
    acc = _dot(h_ref[...], w_ref[...])
    for c in range(acc.shape[1] // group):
        sl = slice(c * group, (c + 1) * group)
        blk = acc[:, sl]
        ms = jnp.mean(blk * blk, axis=-1, keepdims=True)
        o_ref[:, sl] = (blk * lax.rsqrt(ms + EPS) * g_ref[:, sl]).astype(o_ref.dtype)


def _causal_conv(raw_ref, cw_ref, cb_ref, tm, kw):
    y = cb_ref[...]
    for j in range(kw):
        off = HALO - (kw - 1) + j
        y = y + cw_ref[j:j + 1, :] * raw_ref[off:off + tm, :]
    return y


def _shift_halo(raw_ref, tm, tiles_per_seq):
    i = pl.program_id(1)

    @pl.when(i % tiles_per_seq == 0)
    def _():
        raw_ref[0:HALO, :] = jnp.zeros((HALO, raw_ref.shape[1]), F32)

    @pl.when(i % tiles_per_seq != 0)
    def _():
        raw_ref[0:HALO, :] = raw_ref[tm:tm + HALO, :]


def _proj_conv_silu_kernel(h_ref, w_ref, cw_ref, cb_ref, ps_ref, o_ref, raw_ref, *, tm, kw, tiles_per_seq):
    _shift_halo(raw_ref, tm, tiles_per_seq)
    raw_ref[HALO:HALO + tm, :] = _dot(h_ref[...], w_ref[...])
    y = _causal_conv(raw_ref, cw_ref, cb_ref, tm, kw)
    o_ref[...] = (y * _sigmoid(y) * ps_ref[...]).astype(o_ref.dtype)


def _proj(h, w, kind, tm, tn, extra=(), seq=None):
    n, k = h.shape
    m = w.shape[1]
    grid = (m // tn, n // tm)
    h_spec = pl.BlockSpec((tm, k), lambda j, i: (i, 0))
    w_spec = pl.BlockSpec((k, tn), lambda j, i: (0, j))
    o_spec = pl.BlockSpec((tm, tn), lambda j, i: (i, j))
    row_spec = lambda r: pl.BlockSpec((r, tn), lambda j, i: (0, j))
    scratch = []
    sem = ("parallel", "parallel")
    if kind == "plain":
        body, in_specs = _proj_plain_kernel, [h_spec, w_spec]
    elif kind == "sigmoid":
        body, in_specs = _proj_sigmoid_kernel, [h_spec, w_spec]
    elif kind == "headnorm":
        body = functools.partial(_proj_headnorm_kernel, group=B_HEAD_DIM)
        in_specs = [h_spec, w_spec, row_spec(1)]
    elif kind == "conv_silu":
        kw = extra[0].shape[0]
        body = functools.partial(_proj_conv_silu_kernel, tm=tm, kw=kw, tiles_per_seq=seq // tm)
        in_specs = [h_spec, w_spec, row_spec(kw), row_spec(1), row_spec(1)]
        scratch = [pltpu.VMEM((tm + HALO, tn), F32)]
        sem = ("parallel", "arbitrary")
    else:
        raise ValueError(kind)
    return pl.pallas_call(
        body,
        grid=grid,
        in_specs=in_specs,
        out_specs=o_spec,
        out_shape=jax.ShapeDtypeStruct((n, m), BF16),
        scratch_shapes=scratch,
        compiler_params=_params(*sem),
        name="proj_" + kind,
    )(h, w, *extra)


def _gates_kernel(h_ref, w_ref, wt_ref, bc_ref, br_ref, oc_ref, or_ref):
    h = h_ref[...]
    oc_ref[...] = _dot(h, w_ref[...]) + bc_ref[...]
    or_ref[...] = _dot_nt(wt_ref[...], h) + br_ref[...]


def _gates(h, w_if, bias, tm):
    n, k = h.shape
    g = w_if.shape[1]
    return pl.pallas_call(
        _gates_kernel,
        grid=(n // tm,),
        in_specs=[pl.BlockSpec((tm, k), lambda i: (i, 0)),
                  pl.BlockSpec((k, g), lambda i: (0, 0)),
                  pl.BlockSpec((g, k), lambda i: (0, 0)),
                  pl.BlockSpec((1, g), lambda i: (0, 0)),
                  pl.BlockSpec((g, 1), lambda i: (0, 0))],
        out_specs=[pl.BlockSpec((tm, g), lambda i: (i, 0)),
                   pl.BlockSpec((g, tm), lambda i: (0, i))],
        out_shape=[jax.ShapeDtypeStruct((n, g), F32), jax.ShapeDtypeStruct((g, n), F32)],
        compiler_params=_params("parallel"),
        name="proj_gates",
    )(h, w_if, w_if.T, bias.reshape(1, g), bias.reshape(g, 1))


def _log_sigmoid(x):
    return -(jnp.maximum(-x, 0.0) + jnp.log1p(jnp.exp(-jnp.abs(x))))


def _mlstm_kernel(qk_ref, v_ref, gc_ref, gr_ref, so_ref, sg_ref, ng_ref, o_ref,
                  c_ref, n_ref, m_ref):
    L = CHUNK

    @pl.when(pl.program_id(1) == 0)
    def _():
        c_ref[...] = jnp.zeros(c_ref.shape, F32)
        n_ref[...] = jnp.zeros(n_ref.shape, F32)
        m_ref[...] = jnp.zeros(m_ref.shape, F32)

    row = lax.broadcasted_iota(jnp.int32, (L, L), 0)
    col = lax.broadcasted_iota(jnp.int32, (L, L), 1)
    lower = (col <= row)
    tri_lower = lower.astype(F32)
    tri_upper = (row <= col).astype(F32)

    for hd in range(A_HEADS):
        q = qk_ref[:, hd * A_QK_DIM:(hd + 1) * A_QK_DIM]
        k = qk_ref[:, A_QK + hd * A_QK_DIM:A_QK + (hd + 1) * A_QK_DIM]
        v = v_ref[:, hd * A_V_DIM:(hd + 1) * A_V_DIM]
        li_row = gr_ref[hd:hd + 1, :]
        lf_row = _log_sigmoid(gr_ref[A_HEADS + hd:A_HEADS + hd + 1, :])
        li_col = gc_ref[:, hd:hd + 1]
        lf_col = _log_sigmoid(gc_ref[:, A_HEADS + hd:A_HEADS + hd + 1])
        b_row = jnp.dot(lf_row, tri_upper, preferred_element_type=F32, precision=lax.Precision.HIGHEST)
        b_col = jnp.dot(tri_lower, lf_col, preferred_element_type=F32, precision=lax.Precision.HIGHEST)
        m_st = m_ref[hd]
        c_st = c_ref[hd]
        n_st = n_ref[hd]

        dmat = jnp.where(lower, b_col - (b_row - li_row), -jnp.inf)
        inter = b_col + m_st
        m_t = jnp.maximum(jnp.max(dmat, axis=-1, keepdims=True), inter)
        w_in = jnp.exp(dmat - m_t)
        g_inter = jnp.exp(inter - m_t)
        p = _dot_nt(q, k) * w_in
        qf = q.astype(F32)
        num = _dot(p.astype(BF16), v) + g_inter * _dot(q, c_st.astype(BF16))
        den = jnp.sum(p, axis=-1, keepdims=True) + g_inter * jnp.sum(qf * n_st, axis=-1, keepdims=True)
        hh = num / jnp.maximum(jnp.abs(den), jnp.exp(-m_t))

        b_last = b_col[L - 1:L, :]
        ws = b_last - b_col + li_col
        m_new = jnp.maximum(b_last + m_st, jnp.max(ws, axis=0, keepdims=True))
        decay = jnp.exp(b_last + m_st - m_new)
        ws = jnp.exp(ws - m_new)
        kw = k.astype(F32) * ws
        c_ref[hd] = decay * c_st + _dot(kw.T.astype(BF16), v)
        n_ref[hd] = decay * n_st + jnp.sum(kw, axis=0, keepdims=True)
        m_ref[hd] = m_new

        sl = slice(hd * A_V_DIM, (hd + 1) * A_V_DIM)
        ms = jnp.mean(hh * hh, axis=-1, keepdims=True)
        y = hh * lax.rsqrt(ms + EPS) * ng_ref[:, sl]
        gate = so_ref[:, sl].astype(F32) * sg_ref[:, sl].astype(F32)
        o_ref[:, sl] = (gate * y).astype(o_ref.dtype)


def _mlstm(qk, vv, g_col, g_row, sig, norm_g, bsz, seq):
    n = qk.shape[0]
    nc = seq // CHUNK
    rows = lambda b, c: b * nc + c
    return pl.pallas_call(
        _mlstm_kernel,
        grid=(bsz, nc),
        in_specs=[pl.BlockSpec((CHUNK, 2 * A_QK), lambda b, c: (rows(b, c), 0)),
                  pl.BlockSpec((CHUNK, A_V), lambda b, c: (rows(b, c), 0)),
                  pl.BlockSpec((CHUNK, 2 * A_HEADS), lambda b, c: (rows(b, c), 0)),
                  pl.BlockSpec((2 * A_HEADS, CHUNK), lambda b, c: (0, rows(b, c))),
                  pl.BlockSpec((CHUNK, A_V), lambda b, c: (rows(b, c), 0)),
                  pl.BlockSpec((CHUNK, A_V), lambda b, c: (rows(b, c), 1)),
                  pl.BlockSpec((1, A_V), lambda b, c: (0, 0))],
        out_specs=pl.BlockSpec((CHUNK, A_V), lambda b, c: (rows(b, c), 0)),
        out_shape=jax.ShapeDtypeStruct((n, A_V), BF16),
        scratch_shapes=[pltpu.VMEM((A_HEADS, A_QK_DIM, A_V_DIM), F32),
                        pltpu.VMEM((A_HEADS, 1, A_QK_DIM), F32),
                        pltpu.VMEM((A_HEADS, 1, 1), F32)],
        compiler_params=_params("parallel", "arbitrary"),
        name="mlstm",
    )(qk, vv, g_col, g_row, sig, sig, norm_g.reshape(1, A_V))


def _diff_attn_kernel(slope_ref, q_ref, k_ref, v_ref, sg_ref, lp_ref, ng_ref, o_ref,
                      m_ref, l_ref, acc_ref, *, tq, tk, lam_init):
    dh = B_HEAD_DIM
    hd = pl.program_id(1)
    i = pl.program_id(2)
    slope = slope_ref[hd]
    m_ref[...] = jnp.full(m_ref.shape, -jnp.inf, F32)
    l_ref[...] = jnp.zeros(l_ref.shape, F32)
    acc_ref[...] = jnp.zeros(acc_ref.shape, F32)
    key_pos = lax.broadcasted_iota(jnp.int32, (1, tk), 1).astype(F32)

    def step(j, masked):
        start = pl.multiple_of(j * tk, tk)
        kblk = k_ref[pl.ds(start, tk), :]
        vblk = v_ref[pl.ds(start, tk), :]
        bias = slope * (key_pos - ((i - j) * tk).astype(F32))
        for c in range(2):
            s = _dot_nt(q_ref[:, c * dh:(c + 1) * dh], kblk[:, c * dh:(c + 1) * dh]) + bias
            if masked:
                r = lax.broadcasted_iota(jnp.int32, (tq, tk), 0)
                cc = lax.broadcasted_iota(jnp.int32, (tq, tk), 1)
                s = jnp.where(cc <= r, s, -jnp.inf)
            m_old = m_ref[c]
            m_new = jnp.maximum(m_old, jnp.max(s, axis=-1, keepdims=True))
            alpha = jnp.exp(m_old - m_new)
            p = jnp.exp(s - m_new)
            l_ref[c] = alpha * l_ref[c] + jnp.sum(p, axis=-1, keepdims=True)
            acc_ref[c] = alpha * acc_ref[c] + _dot(p.astype(BF16), vblk)
            m_ref[c] = m_new

    def body(j, carry):
        step(j, False)
        return carry

    lax.fori_loop(0, i, body, 0)
    step(i, True)

    lp = lp_ref[...]
    lam = (jnp.exp(jnp.sum(lp[0:1] * lp[1:2], axis=-1, keepdims=True))
           - jnp.exp(jnp.sum(lp[2:3] * lp[3:4], axis=-1, keepdims=True)) + lam_init)
    o = acc_ref[0] / l_ref[0] - lam * (acc_ref[1] / l_ref[1])
    ms = jnp.mean(o * o, axis=-1, keepdims=True)
    y = o * lax.rsqrt(ms + EPS) * ng_ref[...] * (1.0 - lam_init)
    o_ref[...] = (sg_ref[...].astype(F32) * y).astype(o_ref.dtype)


def _diff_attn(qk, vv, sig, lam_params, subln_g, lam_init, bsz, seq, tq):
    n = qk.shape[0]
    hw = 2 * B_HEAD_DIM
    nq = seq // tq
    slopes = jnp.asarray(2.0 ** (-8.0 * np.arange(1, B_HEADS + 1) / B_HEADS), dtype=F32)
    q_cols = B_QK // hw
    v_cols = A_V // hw
    g_cols = 2 * A_V // hw
    kernel = functools.partial(_diff_attn_kernel, tq=tq, tk=tq, lam_init=lam_init)
    grid_spec = pltpu.PrefetchScalarGridSpec(
        num_scalar_prefetch=1,
        grid=(bsz, B_HEADS, nq),
        in_specs=[pl.BlockSpec((tq, hw), lambda b, h, i, s: (b * nq + i, h)),
                  pl.BlockSpec((seq, hw), lambda b, h, i, s: (b, q_cols + h)),
                  pl.BlockSpec((seq, hw), lambda b, h, i, s: (b, v_cols + h)),
                  pl.BlockSpec((tq, hw), lambda b, h, i, s: (b * nq + i, g_cols + h)),
                  pl.BlockSpec((4, B_HEAD_DIM), lambda b, h, i, s: (0, 0)),
                  pl.BlockSpec((1, hw), lambda b, h, i, s: (0, 0))],
        out_specs=pl.BlockSpec((tq, hw), lambda b, h, i, s: (b * nq + i, h)),
        scratch_shapes=[pltpu.VMEM((2, tq, 1), F32),
                        pltpu.VMEM((2, tq, 1), F32),
                        pltpu.VMEM((2, tq, hw), F32)],
    )
    return pl.pallas_call(
        kernel,
        grid_spec=grid_spec,
        out_shape=jax.ShapeDtypeStruct((n, B_V), BF16),
        compiler_params=_params("parallel", "parallel", "parallel"),
        name="diff_attn",
    )(slopes, qk, qk, vv, sig, lam_params, subln_g.reshape(1, hw))


def _out_proj_kernel(ya_ref, yb_ref, w_ref, x_ref, g_ref, x1_ref, h2_ref):
    y = (ya_ref[...].astype(F32) + yb_ref[...].astype(F32)).astype(BF16)
    x1 = x_ref[...] + _dot(y, w_ref[...])
    x1_ref[...] = x1
    ms = jnp.mean(x1 * x1, axis=-1, keepdims=True)
    h2_ref[...] = (x1 * lax.rsqrt(ms + EPS) * g_ref[...]).astype(h2_ref.dtype)


def _out_proj(ya, yb, w, x, g, tm):
    n, d = x.shape
    row = pl.BlockSpec((tm, d), lambda i: (i, 0))
    return pl.pallas_call(
        _out_proj_kernel,
        grid=(n // tm,),
        in_specs=[row, row, pl.BlockSpec((d, d), lambda i: (0, 0)), row,
                  pl.BlockSpec((1, d), lambda i: (0, 0))],
        out_specs=[row, row],
        out_shape=[jax.ShapeDtypeStruct((n, d), F32), jax.ShapeDtypeStruct((n, d), BF16)],
        compiler_params=_params("parallel"),
        name="out_proj",
    )(ya, yb, w, x, g.reshape(1, d))


def _ffn_up_kernel(h_ref, wg_ref, wv_ref, cwg_ref, cwv_ref, cbg_ref, cbv_ref, o_ref,
                   rg_ref, rv_ref, *, tm, kw, tiles_per_seq):
    _shift_halo(rg_ref, tm, tiles_per_seq)
    _shift_halo(rv_ref, tm, tiles_per_seq)
    h = h_ref[...]
    rg_ref[HALO:HALO + tm, :] = _dot(h, wg_ref[...])
    rv_ref[HALO:HALO + tm, :] = _dot(h, wv_ref[...])
    ug = _causal_conv(rg_ref, cwg_ref, cbg_ref, tm, kw)
    uv = _causal_conv(rv_ref, cwv_ref, cbv_ref, tm, kw)
    o_ref[...] = (ug * _sigmoid(ug) * uv).astype(o_ref.dtype)


def _ffn_up(h, w_up, conv_w, conv_b, seq, tm, tn):
    n, k = h.shape
    nj = D_FF // tn
    kw = conv_w.shape[0]
    gate = lambda r: pl.BlockSpec((r, tn), lambda j, i: (0, j))
    val = lambda r: pl.BlockSpec((r, tn), lambda j, i: (0, nj + j))
    kernel = functools.partial(_ffn_up_kernel, tm=tm, kw=kw, tiles_per_seq=seq // tm)
    return pl.pallas_call(
        kernel,
        grid=(nj, n // tm),
        in_specs=[pl.BlockSpec((tm, k), lambda j, i: (i, 0)),
                  gate(k), val(k), gate(kw), val(kw), gate(1), val(1)],
        out_specs=pl.BlockSpec((tm, tn), lambda j, i: (i, j)),
        out_shape=jax.ShapeDtypeStruct((n, D_FF), BF16),
        scratch_shapes=[pltpu.VMEM((tm + HALO, tn), F32), pltpu.VMEM((tm + HALO, tn), F32)],
        compiler_params=_params("parallel", "arbitrary"),
        name="ffn_up",
    )(h, w_up, w_up, conv_w, conv_w, conv_b, conv_b)


def _ffn_down_kernel(a_ref, w_ref, x_ref, o_ref):
    k = pl.program_id(1)

    @pl.when(k == 0)
    def _():
        o_ref[...] = x_ref[...]

    o_ref[...] += _dot(a_ref[...], w_ref[...])


def _ffn_down(act, w, x1, tm, tk):
    n, d = x1.shape
    return pl.pallas_call(
        _ffn_down_kernel,
        grid=(n // tm, D_FF // tk),
        in_specs=[pl.BlockSpec((tm, tk), lambda i, k: (i, k)),
                  pl.BlockSpec((tk, d), lambda i, k: (k, 0)),
                  pl.BlockSpec((tm, d), lambda i, k: (i, 0))],
        out_specs=pl.BlockSpec((tm, d), lambda i, k: (i, 0)),
        out_shape=jax.ShapeDtypeStruct((n, d), F32),
        compiler_params=_params("parallel", "arbitrary"),
        name="ffn_down",
    )(act, w, x1)


def _layer(x2, bsz, seq, l, norm1_g, w_in, if_bias, qk_conv_w, qk_conv_b, mlstm_norm_g, q_norm_g,
           k_norm_g, diff_lambda, subln_g, w_out, norm2_g, w_up, ffn_conv_w, ffn_conv_b, w_down):
    n = bsz * seq
    tm = min(1024, seq)
    off = np.cumsum((0,) + SPLIT_SIZES)
    cols = lambda g: w_in[:, off[g]:off[g + 1]]
    w_aqk = cols(0).astype(BF16)
    w_plain = jnp.concatenate([cols(1), cols(6)], axis=1).astype(BF16)
    w_sig = jnp.concatenate([cols(2), cols(7), cols(8)], axis=1).astype(BF16)
    w_if = cols(3).astype(BF16)
    w_bqk = jnp.concatenate([cols(4), cols(5)], axis=1).astype(BF16)

    h = _rmsnorm(x2, norm1_g, min(512, seq))

    q_scale = jnp.concatenate([jnp.full((A_QK,), A_QK_DIM ** -0.5, F32), jnp.ones((A_QK,), F32)])
    qk_m = _proj(h, w_aqk, "conv_silu", tm, 512,
                 extra=(qk_conv_w, qk_conv_b.reshape(1, -1), q_scale.reshape(1, -1)), seq=seq)
    vv = _proj(h, w_plain, "plain", tm, 1024)
    sig = _proj(h, w_sig, "sigmoid", tm, 1024)
    reps = B_QK // B_HEAD_DIM
    qk_gain = jnp.concatenate([jnp.tile(q_norm_g * (B_HEAD_DIM ** -0.5), reps), jnp.tile(k_norm_g, reps)])
    qk_d = _proj(h, w_bqk, "headnorm", tm, 1024, extra=(qk_gain.reshape(1, -1),))
    g_col, g_row = _gates(h, w_if, if_bias, tm)

    y_a = _mlstm(qk_m, vv, g_col, g_row, sig, mlstm_norm_g, bsz, seq)
    lam_init = 0.8 - 0.6 * math.exp(-0.3 * l)
    y_b = _diff_attn(qk_d, vv, sig, diff_lambda, subln_g, lam_init, bsz, seq, min(512, seq))

    x1, h2 = _out_proj(y_a, y_b, w_out.astype(BF16), x2, norm2_g, min(256, seq))
    act = _ffn_up(h2, w_up.astype(BF16), ffn_conv_w, ffn_conv_b.reshape(1, -1), seq, tm, 512)
    return _ffn_down(act, w_down.astype(BF16), x1, min(512, seq), D_FF // 4)


def kernel(x, norm1_g, w_in, if_bias, qk_conv_w, qk_conv_b, mlstm_norm_g, q_norm_g, k_norm_g,
           diff_lambda, subln_g, w_out, norm2_g, w_up, ffn_conv_w, ffn_conv_b, w_down):
    bsz, seq, d = x.shape
    x2 = x.reshape(bsz * seq, d)
    for l in range(norm1_g.shape[0]):
        x2 = _layer(x2, bsz, seq, l, norm1_g[l], w_in[l], if_bias[l], qk_conv_w[l], qk_conv_b[l],
                    mlstm_norm_g[l], q_norm_g[l], k_norm_g[l], diff_lambda[l], subln_g[l], w_out[l],
                    norm2_g[l], w_up[l], ffn_conv_w[l], ffn_conv_b[l], w_down[l])
    return x2.reshape(bsz, seq, d)
```

```python
import functools
import math

import numpy as np
import jax
import jax.numpy as jnp
from jax import lax
from jax.experimental import pallas as pl
from jax.experimental.pallas import tpu as pltpu

F32 = jnp.float32
BF16 = jnp.bfloat16

D_MODEL = 2048
A_HEADS = 4
A_QK_DIM = 256
A_V_DIM = 512
A_CONV = 4
CHUNK = 128
B_HEADS = 8
B_HEAD_DIM = 128
D_FF = 5632
FFN_CONV = 3
EPS = 1e-6

A_QK = A_HEADS * A_QK_DIM
A_V = A_HEADS * A_V_DIM
B_QK = B_HEADS * 2 * B_HEAD_DIM
B_V = B_HEADS * 2 * B_HEAD_DIM
SPLIT_SIZES = (2 * A_QK, A_V, A_V, 2 * A_HEADS, B_QK, B_QK, B_V, D_MODEL, D_MODEL)

SUBLANES = 8
LANES = 128
LOG2E = math.log2(math.e)
HALO = SUBLANES
VMEM_LIMIT = 48 * 1024 * 1024


def _params(*sem):
    return pltpu.CompilerParams(dimension_semantics=sem, vmem_limit_bytes=VMEM_LIMIT)


def _dot(a, b):
    return jnp.dot(a, b, preferred_element_type=F32)


def _dot_nt(a, b):
    return lax.dot_general(a, b, (((1,), (1,)), ((), ())), preferred_element_type=F32)


def _sigmoid(x):
    return 1.0 / (1.0 + jnp.exp(-x))


def _rmsnorm_kernel(x_ref, g_ref, o_ref):
    x = x_ref[...]
    ms = jnp.mean(x * x, axis=-1, keepdims=True)
    o_ref[...] = (x * lax.rsqrt(ms + EPS) * g_ref[...]).astype(o_ref.dtype)


def _rmsnorm(x, g, tm):
    n, d = x.shape
    return pl.pallas_call(
        _rmsnorm_kernel,
        grid=(n // tm,),
        in_specs=[pl.BlockSpec((tm, d), lambda i: (i, 0)),
                  pl.BlockSpec((1, d), lambda i: (0, 0))],
        out_specs=pl.BlockSpec((tm, d), lambda i: (i, 0)),
        out_shape=jax.ShapeDtypeStruct((n, d), BF16),
        compiler_params=_params("parallel"),
        name="rmsnorm1",
    )(x, g.reshape(1, d))


def _proj_plain_kernel(h_ref, w_ref, o_ref):
    o_ref[...] = _dot(h_ref[...], w_ref[...]).astype(o_ref.dtype)


def _proj_sigmoid_kernel(h_ref, w_ref, o_ref):
    o_ref[...] = _sigmoid(_dot(h_ref[...], w_ref[...])).astype(o_ref.dtype)


def _proj_headnorm_kernel(h_ref, w_ref, g_ref, o_ref, *, group):
    acc = _dot(h_ref[...], w_ref[...])
    for c in range(acc.shape[1] // group):
        sl = slice(c * group, (c + 1) * group)
        blk = acc[:, sl]
        ms = jnp.mean(blk * blk, axis=-1, keepdims=True)
        o_ref[:, sl] = (blk * lax.rsqrt(ms + EPS) * g_ref[:, sl]).astype(o_ref.dtype)


def _causal_conv(raw_ref, cw_ref, cb_ref, tm, kw):
    y = cb_ref[...]
    for j in range(kw):
        off = HALO - (kw - 1) + j
        y = y + cw_ref[j:j + 1, :] * raw_ref[off:off + tm, :]
    return y


def _shift_halo(raw_ref, tm, tiles_per_seq):
    i = pl.program_id(1)

    @pl.when(i % tiles_per_seq == 0)
    def _():
        raw_ref[0:HALO, :] = jnp.zeros((HALO, raw_ref.shape[1]), F32)

    @pl.when(i % tiles_per_seq != 0)
    def _():
        raw_ref[0:HALO, :] = raw_ref[tm:tm + HALO, :]


def _proj_conv_silu_kernel(h_ref, w_ref, cw_ref, cb_ref, ps_ref, o_ref, raw_ref, *, tm, kw, tiles_per_seq):
    _shift_halo(raw_ref, tm, tiles_per_seq)
    raw_ref[HALO:HALO + tm, :] = _dot(h_ref[...], w_ref[...])
    y = _causal_conv(raw_ref, cw_ref, cb_ref, tm, kw)
    o_ref[...] = (y * _sigmoid(y) * ps_ref[...]).astype(o_ref.dtype)


def _proj(h, w, kind, tm, tn, extra=(), seq=None):
    n, k = h.shape
    m = w.shape[1]
    grid = (m // tn, n // tm)
    h_spec = pl.BlockSpec((tm, k), lambda j, i: (i, 0))
    w_spec = pl.BlockSpec((k, tn), lambda j, i: (0, j))
    o_spec = pl.BlockSpec((tm, tn), lambda j, i: (i, j))
    row_spec = lambda r: pl.BlockSpec((r, tn), lambda j, i: (0, j))
    scratch = []
    sem = ("parallel", "parallel")
    if kind == "plain":
        body, in_specs = _proj_plain_kernel, [h_spec, w_spec]
    elif kind == "sigmoid":
        body, in_specs = _proj_sigmoid_kernel, [h_spec, w_spec]
    elif kind == "headnorm":
        body = functools.partial(_proj_headnorm_kernel, group=B_HEAD_DIM)
        in_specs = [h_spec, w_spec, row_spec(1)]
    elif kind == "conv_silu":
        kw = extra[0].shape[0]
        body = functools.partial(_proj_conv_silu_kernel, tm=tm, kw=kw, tiles_per_seq=seq // tm)
        in_specs = [h_spec, w_spec, row_spec(kw), row_spec(1), row_spec(1)]
        scratch = [pltpu.VMEM((tm + HALO, tn), F32)]
        sem = ("parallel", "arbitrary")
    else:
        raise ValueError(kind)
    return pl.pallas_call(
        body,
        grid=grid,
        in_specs=in_specs,
        out_specs=o_spec,
        out_shape=jax.ShapeDtypeStruct((n, m), BF16),
        scratch_shapes=scratch,
        compiler_params=_params(*sem),
        name="proj_" + kind,
    )(h, w, *extra)


def _gates_kernel(h_ref, w_ref, wt_ref, bc_ref, br_ref, oc_ref, or_ref):
    h = h_ref[...]
    oc_ref[...] = _dot(h, w_ref[...]) + bc_ref[...]
    or_ref[...] = _dot_nt(wt_ref[...], h) + br_ref[...]


def _gates(h, w_if, bias, tm):
    n, k = h.shape
    g = w_if.shape[1]
    return pl.pallas_call(
        _gates_kernel,
        grid=(n // tm,),
        in_specs=[pl.BlockSpec((tm, k), lambda i: (i, 0)),
                  pl.BlockSpec((k, g), lambda i: (0, 0)),
                  pl.BlockSpec((g, k), lambda i: (0, 0)),
                  pl.BlockSpec((1, g), lambda i: (0, 0)),
                  pl.BlockSpec((g, 1), lambda i: (0, 0))],
        out_specs=[pl.BlockSpec((tm, g), lambda i: (i, 0)),
                   pl.BlockSpec((g, tm), lambda i: (0, i))],
        out_shape=[jax.ShapeDtypeStruct((n, g), F32), jax.ShapeDtypeStruct((g, n), F32)],
        compiler_params=_params("parallel"),
        name="proj_gates",
    )(h, w_if, w_if.T, bias.reshape(1, g), bias.reshape(g, 1))


def _log_sigmoid(x):
    return -(jnp.maximum(-x, 0.0) + jnp.log1p(jnp.exp(-jnp.abs(x))))


def _mlstm_kernel(qk_ref, v_ref, gc_ref, gr_ref, so_ref, sg_ref, ng_ref, o_ref,
                  c_ref, n_ref, m_ref):
    L = CHUNK

    @pl.when(pl.program_id(1) == 0)
    def _():
        c_ref[...] = jnp.zeros(c_ref.shape, F32)
        n_ref[...] = jnp.zeros(n_ref.shape, F32)
        m_ref[...] = jnp.zeros(m_ref.shape, F32)

    row = lax.broadcasted_iota(jnp.int32, (L, L), 0)
    col = lax.broadcasted_iota(jnp.int32, (L, L), 1)
    lower = (col <= row)
    tri_lower = lower.astype(F32)
    tri_upper = (row <= col).astype(F32)

    for hd in range(A_HEADS):
        q = qk_ref[:, hd * A_QK_DIM:(hd + 1) * A_QK_DIM]
        k = qk_ref[:, A_QK + hd * A_QK_DIM:A_QK + (hd + 1) * A_QK_DIM]
        v = v_ref[:, hd * A_V_DIM:(hd + 1) * A_V_DIM]
        li_row = gr_ref[hd:hd + 1, :]
        lf_row = _log_sigmoid(gr_ref[A_HEADS + hd:A_HEADS + hd + 1, :])
        li_col = gc_ref[:, hd:hd + 1]
        lf_col = _log_sigmoid(gc_ref[:, A_HEADS + hd:A_HEADS + hd + 1])
        b_row = jnp.dot(lf_row, tri_upper, preferred_element_type=F32, precision=lax.Precision.HIGHEST)
        b_col = jnp.dot(tri_lower, lf_col, preferred_element_type=F32, precision=lax.Precision.HIGHEST)
        m_st = m_ref[hd]
        c_st = c_ref[hd]
        n_st = n_ref[hd]

        dmat = jnp.where(lower, b_col - (b_row - li_row), -jnp.inf)
        inter = b_col + m_st
        m_t = jnp.maximum(jnp.max(dmat, axis=-1, keepdims=True), inter)
        w_in = jnp.exp(dmat - m_t)
        g_inter = jnp.exp(inter - m_t)
        p = _dot_nt(q, k) * w_in
        qf = q.astype(F32)
        num = _dot(p.astype(BF16), v) + g_inter * _dot(q, c_st.astype(BF16))
        den = jnp.sum(p, axis=-1, keepdims=True) + g_inter * jnp.sum(qf * n_st, axis=-1, keepdims=True)
        hh = num / jnp.maximum(jnp.abs(den), jnp.exp(-m_t))

        b_last = b_col[L - 1:L, :]
        ws = b_last - b_col + li_col
        m_new = jnp.maximum(b_last + m_st, jnp.max(ws, axis=0, keepdims=True))
        decay = jnp.exp(b_last + m_st - m_new)
        ws = jnp.exp(ws - m_new)
        kw = k.astype(F32) * ws
        c_ref[hd] = decay * c_st + _dot(kw.T.astype(BF16), v)
        n_ref[hd] = decay * n_st + jnp.sum(kw, axis=0, keepdims=True)
        m_ref[hd] = m_new

        sl = slice(hd * A_V_DIM, (hd + 1) * A_V_DIM)
        ms = jnp.mean(hh * hh, axis=-1, keepdims=True)
        y = hh * lax.rsqrt(ms + EPS) * ng_ref[:, sl]
        gate = so_ref[:, sl].astype(F32) * sg_ref[:, sl].astype(F32)
        o_ref[:, sl] = (gate * y).astype(o_ref.dtype)


def _mlstm(qk, vv, g_col, g_row, sig, norm_g, bsz, seq):
    n = qk.shape[0]
    nc = seq // CHUNK
    rows = lambda b, c: b * nc + c
    return pl.pallas_call(
        _mlstm_kernel,
        grid=(bsz, nc),
        in_specs=[pl.BlockSpec((CHUNK, 2 * A_QK), lambda b, c: (rows(b, c), 0)),
                  pl.BlockSpec((CHUNK, A_V), lambda b, c: (rows(b, c), 0)),
                  pl.BlockSpec((CHUNK, 2 * A_HEADS), lambda b, c: (rows(b, c), 0)),
                  pl.BlockSpec((2 * A_HEADS, CHUNK), lambda b, c: (0, rows(b, c))),
                  pl.BlockSpec((CHUNK, A_V), lambda b, c: (rows(b, c), 0)),
                  pl.BlockSpec((CHUNK, A_V), lambda b, c: (rows(b, c), 1)),
                  pl.BlockSpec((1, A_V), lambda b, c: (0, 0))],
        out_specs=pl.BlockSpec((CHUNK, A_V), lambda b, c: (rows(b, c), 0)),
        out_shape=jax.ShapeDtypeStruct((n, A_V), BF16),
        scratch_shapes=[pltpu.VMEM((A_HEADS, A_QK_DIM, A_V_DIM), F32),
                        pltpu.VMEM((A_HEADS, 1, A_QK_DIM), F32),
                        pltpu.VMEM((A_HEADS, 1, 1), F32)],
        compiler_params=_params("parallel", "arbitrary"),
        name="mlstm",
    )(qk, vv, g_col, g_row, sig, sig, norm_g.reshape(1, A_V))


def _diff_attn_kernel(slope_ref, q_ref, k_ref, v_ref, sg_ref, lp_ref, ng_ref, o_ref,
                      m_ref, l_ref, acc_ref, *, tq, tk, lam_init):
    dh = B_HEAD_DIM
    hw = 2 * dh
    hd = pl.program_id(1)
    i = pl.program_id(2)
    slope = slope_ref[hd]
    m_ref[...] = jnp.full(m_ref.shape, -jnp.inf, F32)
    l_ref[...] = jnp.zeros(l_ref.shape, F32)
    acc_ref[...] = jnp.zeros(acc_ref.shape, F32)
    key_pos = lax.broadcasted_iota(jnp.int32, (1, tk), 1).astype(F32)

    def step(j, masked):
        start = pl.multiple_of(j * tk, tk)
        kblk = k_ref[pl.ds(start, tk), :]
        vblk = v_ref[pl.ds(start, tk), :]
        bias = slope * (key_pos - ((i - j) * tk).astype(F32))
        ps, alphas = [], []
        for c in range(2):
            s = _dot_nt(q_ref[:, c * dh:(c + 1) * dh], kblk[:, c * dh:(c + 1) * dh]) + bias
            if masked:
                r = lax.broadcasted_iota(jnp.int32, (tq, tk), 0)
                cc = lax.broadcasted_iota(jnp.int32, (tq, tk), 1)
                s = jnp.where(cc <= r, s, -jnp.inf)
            m_old = m_ref[c]
            m_new = jnp.maximum(m_old, jnp.max(s, axis=-1, keepdims=True))
            alpha = jnp.exp2(m_old - m_new)
            p = jnp.exp2(s - pltpu.repeat(m_new, tk // LANES, axis=1))
            l_ref[c] = alpha * l_ref[c] + jnp.sum(p, axis=-1, keepdims=True)
            m_ref[c] = m_new
            ps.append(p.astype(BF16))
            alphas.append(alpha)
        pv = _dot(jnp.concatenate(ps, axis=0), vblk)
        for c in range(2):
            acc_ref[c] = pltpu.repeat(alphas[c], hw // LANES, axis=1) * acc_ref[c] + pv[c * tq:(c + 1) * tq]

    def body(j, carry):
        step(j, False)
        return carry

    lax.fori_loop(0, i, body, 0)
    step(i, True)

    lp = lp_ref[...]
    lam = (jnp.exp(jnp.sum(lp[0:1] * lp[1:2], axis=-1, keepdims=True))
           - jnp.exp(jnp.sum(lp[2:3] * lp[3:4], axis=-1, keepdims=True)) + lam_init)
    inv0 = 1.0 / l_ref[0]
    inv1 = lam / l_ref[1]
    o = (acc_ref[0] * pltpu.repeat(inv0, hw // LANES, axis=1)
         - acc_ref[1] * pltpu.repeat(inv1, hw // LANES, axis=1))
    ms = jnp.mean(o * o, axis=-1, keepdims=True)
    y = o * lax.rsqrt(ms + EPS) * ng_ref[...] * (1.0 - lam_init)
    o_ref[...] = (sg_ref[...].astype(F32) * y).astype(o_ref.dtype)


def _diff_attn(qk, vv, sig, lam_params, subln_g, lam_init, bsz, seq, tq):
    n = qk.shape[0]
    hw = 2 * B_HEAD_DIM
    nq = seq // tq
    slopes = jnp.asarray(2.0 ** (-8.0 * np.arange(1, B_HEADS + 1) / B_HEADS) * LOG2E, dtype=F32)
    q_cols = B_QK // hw
    v_cols = A_V // hw
    g_cols = 2 * A_V // hw
    kernel = functools.partial(_diff_attn_kernel, tq=tq, tk=tq, lam_init=lam_init)
    grid_spec = pltpu.PrefetchScalarGridSpec(
        num_scalar_prefetch=1,
        grid=(bsz, B_HEADS, nq),
        in_specs=[pl.BlockSpec((tq, hw), lambda b, h, i, s: (b * nq + i, h)),
                  pl.BlockSpec((seq, hw), lambda b, h, i, s: (b, q_cols + h)),
                  pl.BlockSpec((seq, hw), lambda b, h, i, s: (b, v_cols + h)),
                  pl.BlockSpec((tq, hw), lambda b, h, i, s: (b * nq + i, g_cols + h)),
                  pl.BlockSpec((4, B_HEAD_DIM), lambda b, h, i, s: (0, 0)),
                  pl.BlockSpec((1, hw), lambda b, h, i, s: (0, 0))],
        out_specs=pl.BlockSpec((tq, hw), lambda b, h, i, s: (b * nq + i, h)),
        scratch_shapes=[pltpu.VMEM((2, tq, LANES), F32),
                        pltpu.VMEM((2, tq, LANES), F32),
                        pltpu.VMEM((2, tq, hw), F32)],
    )
    return pl.pallas_call(
        kernel,
        grid_spec=grid_spec,
        out_shape=jax.ShapeDtypeStruct((n, B_V), BF16),
        compiler_params=_params("parallel", "parallel", "parallel"),
        name="diff_attn",
    )(slopes, qk, qk, vv, sig, lam_params, subln_g.reshape(1, hw))


def _out_proj_kernel(ya_ref, yb_ref, w_ref, x_ref, g_ref, x1_ref, h2_ref):
    y = (ya_ref[...].astype(F32) + yb_ref[...].astype(F32)).astype(BF16)
    x1 = x_ref[...] + _dot(y, w_ref[...])
    x1_ref[...] = x1
    ms = jnp.mean(x1 * x1, axis=-1, keepdims=True)
    h2_ref[...] = (x1 * lax.rsqrt(ms + EPS) * g_ref[...]).astype(h2_ref.dtype)


def _out_proj(ya, yb, w, x, g, tm):
    n, d = x.shape
    row = pl.BlockSpec((tm, d), lambda i: (i, 0))
    return pl.pallas_call(
        _out_proj_kernel,
        grid=(n // tm,),
        in_specs=[row, row, pl.BlockSpec((d, d), lambda i: (0, 0)), row,
                  pl.BlockSpec((1, d), lambda i: (0, 0))],
        out_specs=[row, row],
        out_shape=[jax.ShapeDtypeStruct((n, d), F32), jax.ShapeDtypeStruct((n, d), BF16)],
        compiler_params=_params("parallel"),
        name="out_proj",
    )(ya, yb, w, x, g.reshape(1, d))


def _ffn_up_kernel(h_ref, wg_ref, wv_ref, cwg_ref, cwv_ref, cbg_ref, cbv_ref, o_ref,
                   rg_ref, rv_ref, *, tm, kw, tiles_per_seq):
    _shift_halo(rg_ref, tm, tiles_per_seq)
    _shift_halo(rv_ref, tm, tiles_per_seq)
    h = h_ref[...]
    rg_ref[HALO:HALO + tm, :] = _dot(h, wg_ref[...])
    rv_ref[HALO:HALO + tm, :] = _dot(h, wv_ref[...])
    ug = _causal_conv(rg_ref, cwg_ref, cbg_ref, tm, kw)
    uv = _causal_conv(rv_ref, cwv_ref, cbv_ref, tm, kw)
    o_ref[...] = (ug * _sigmoid(ug) * uv).astype(o_ref.dtype)


def _ffn_up(h, w_up, conv_w, conv_b, seq, tm, tn):
    n, k = h.shape
    nj = D_FF // tn
    kw = conv_w.shape[0]
    gate = lambda r: pl.BlockSpec((r, tn), lambda j, i: (0, j))
    val = lambda r: pl.BlockSpec((r, tn), lambda j, i: (0, nj + j))
    kernel = functools.partial(_ffn_up_kernel, tm=tm, kw=kw, tiles_per_seq=seq // tm)
    return pl.pallas_call(
        kernel,
        grid=(nj, n // tm),
        in_specs=[pl.BlockSpec((tm, k), lambda j, i: (i, 0)),
                  gate(k), val(k), gate(kw), val(kw), gate(1), val(1)],
        out_specs=pl.BlockSpec((tm, tn), lambda j, i: (i, j)),
        out_shape=jax.ShapeDtypeStruct((n, D_FF), BF16),
        scratch_shapes=[pltpu.VMEM((tm + HALO, tn), F32), pltpu.VMEM((tm + HALO, tn), F32)],
        compiler_params=_params("parallel", "arbitrary"),
        name="ffn_up",
    )(h, w_up, w_up, conv_w, conv_w, conv_b, conv_b)


def _ffn_down_kernel(a_ref, w_ref, x_ref, o_ref):
    k = pl.program_id(1)

    @pl.when(k == 0)
    def _():
        o_ref[...] = x_ref[...]

    o_ref[...] += _dot(a_ref[...], w_ref[...])


def _ffn_down(act, w, x1, tm, tk):
    n, d = x1.shape
    return pl.pallas_call(
        _ffn_down_kernel,
        grid=(n // tm, D_FF // tk),
        in_specs=[pl.BlockSpec((tm, tk), lambda i, k: (i, k)),
                  pl.BlockSpec((tk, d), lambda i, k: (k, 0)),
                  pl.BlockSpec((tm, d), lambda i, k: (i, 0))],
        out_specs=pl.BlockSpec((tm, d), lambda i, k: (i, 0)),
        out_shape=jax.ShapeDtypeStruct((n, d), F32),
        compiler_params=_params("parallel", "arbitrary"),
        name="ffn_down",
    )(act, w, x1)


def _layer(x2, bsz, seq, l, norm1_g, w_in, if_bias, qk_conv_w, qk_conv_b, mlstm_norm_g, q_norm_g,
           k_norm_g, diff_lambda, subln_g, w_out, norm2_g, w_up, ffn_conv_w, ffn_conv_b, w_down):
    n = bsz * seq
    tm = min(1024, seq)
    off = np.cumsum((0,) + SPLIT_SIZES)
    cols = lambda g: w_in[:, off[g]:off[g + 1]]
    w_aqk = cols(0).astype(BF16)
    w_plain = jnp.concatenate([cols(1), cols(6)], axis=1).astype(BF16)
    w_sig = jnp.concatenate([cols(2), cols(7), cols(8)], axis=1).astype(BF16)
    w_if = cols(3).astype(BF16)
    w_bqk = jnp.concatenate([cols(4), cols(5)], axis=1).astype(BF16)

    h = _rmsnorm(x2, norm1_g, min(512, seq))

    q_scale = jnp.concatenate([jnp.full((A_QK,), A_QK_DIM ** -0.5, F32), jnp.ones((A_QK,), F32)])
    qk_m = _proj(h, w_aqk, "conv_silu", tm, 512,
                 extra=(qk_conv_w, qk_conv_b.reshape(1, -1), q_scale.reshape(1, -1)), seq=seq)
    vv = _proj(h, w_plain, "plain", tm, 1024)
    sig = _proj(h, w_sig, "sigmoid", tm, 1024)
    reps = B_QK // B_HEAD_DIM
    qk_gain = jnp.concatenate([jnp.tile(q_norm_g * (B_HEAD_DIM ** -0.5 * LOG2E), reps), jnp.tile(k_norm_g, reps)])
    qk_d = _proj(h, w_bqk, "headnorm", tm, 1024, extra=(qk_gain.reshape(1, -1),))
    g_col, g_row = _gates(h, w_if, if_bias, tm)

    y_a = _mlstm(qk_m, vv, g_col, g_row, sig, mlstm_norm_g, bsz, seq)
    lam_init = 0.8 - 0.6 * math.exp(-0.3 * l)
    y_b = _diff_attn(qk_d, vv, sig, diff_lambda, subln_g, lam_init, bsz, seq, min(512, seq))

    x1, h2 = _out_proj(y_a, y_b, w_out.astype(BF16), x2, norm2_g, min(256, seq))
    act = _ffn_up(h2, w_up.astype(BF16), ffn_conv_w, ffn_conv_b.reshape(1, -1), seq, tm, 512)
    return _ffn_down(act, w_down.astype(BF16), x1, min(512, seq), D_FF // 4)


def kernel(x, norm1_g, w_in, if_bias, qk_conv_w, qk_conv_b, mlstm_norm_g, q_norm_g, k_norm_g,
           diff_lambda, subln_g, w_out, norm2_g, w_up, ffn_conv_w, ffn_conv_b, w_down):
    bsz, seq, d = x.shape
    x2 = x.reshape(bsz * seq, d)
    for l in range(norm1_g.shape[0]):
        x2 = _layer(x2, bsz, seq, l, norm1_g[l], w_in[l], if_bias[l], qk_conv_w[l], qk_conv_b[l],
                    mlstm_norm_g[l], q_norm_g[l], k_norm_g[l], diff_lambda[l], subln_g[l], w_out[l],
                    norm2_g[l], w_up[l], ffn_conv_w[l], ffn_conv_b[l], w_down[l])
    return x2.reshape(bsz, seq, d)
```

```python
import functools
import math

import numpy as np
import jax
import jax.numpy as jnp
from jax import lax
from jax.experimental import pallas as pl
from jax.experimental.pallas import tpu as pltpu

F32 = jnp.float32
BF16 = jnp.bfloat16

D_MODEL = 2048
A_HEADS = 4
A_QK_DIM = 256
A_V_DIM = 512
A_CONV = 4
CHUNK = 128
B_HEADS = 8
B_HEAD_DIM = 128
D_FF = 5632
FFN_CONV = 3
EPS = 1e-6

A_QK = A_HEADS * A_QK_DIM
A_V = A_HEADS * A_V_DIM
B_QK = B_HEADS * 2 * B_HEAD_DIM
B_V = B_HEADS * 2 * B_HEAD_DIM
SPLIT_SIZES = (2 * A_QK, A_V, A_V, 2 * A_HEADS, B_QK, B_QK, B_V, D_MODEL, D_MODEL)

SUBLANES = 8
LANES = 128
LOG2E = math.log2(math.e)
HALO = SUBLANES
VMEM_LIMIT = 56 * 1024 * 1024


def _params(*sem):
    return pltpu.CompilerParams(dimension_semantics=sem, vmem_limit_bytes=VMEM_LIMIT)


def _dot(a, b):
    return jnp.dot(a, b, preferred_element_type=F32)


def _dot_nt(a, b):
    return lax.dot_general(a, b, (((1,), (1,)), ((), ())), preferred_element_type=F32)


def _sigmoid(x):
    return 1.0 / (1.0 + jnp.exp(-x))


def _lane_tile(x, width):
    return jnp.concatenate([x] * (width // LANES), axis=1)


def _rmsnorm_gates_kernel(x_ref, g_ref, w_ref, wt_ref, bc_ref, br_ref, o_ref, oc_ref, or_ref):
    x = x_ref[...]
    ms = jnp.mean(x * x, axis=-1, keepdims=True)
    h = (x * lax.rsqrt(ms + EPS) * g_ref[...]).astype(o_ref.dtype)
    o_ref[...] = h
    oc_ref[...] = _dot(h, w_ref[...]) + bc_ref[...]
    or_ref[...] = _dot_nt(wt_ref[...], h) + br_ref[...]


def _rmsnorm_gates(x, g, w_if, bias, tm):
    n, d = x.shape
    ng = w_if.shape[1]
    const = lambda r, c: pl.BlockSpec((r, c), lambda i: (0, 0))
    return pl.pallas_call(
        _rmsnorm_gates_kernel,
        grid=(n // tm,),
        in_specs=[pl.BlockSpec((tm, d), lambda i: (i, 0)), const(1, d),
                  const(d, ng), const(ng, d), const(1, ng), const(ng, 1)],
        out_specs=[pl.BlockSpec((tm, d), lambda i: (i, 0)),
                   pl.BlockSpec((tm, ng), lambda i: (i, 0)),
                   pl.BlockSpec((ng, tm), lambda i: (0, i))],
        out_shape=[jax.ShapeDtypeStruct((n, d), BF16),
                   jax.ShapeDtypeStruct((n, ng), F32), jax.ShapeDtypeStruct((ng, n), F32)],
        compiler_params=_params("parallel"),
        name="rmsnorm_gates",
    )(x, g.reshape(1, d), w_if, w_if.T, bias.reshape(1, ng), bias.reshape(ng, 1))


def _proj_plain_kernel(h_ref, w_ref, o_ref):
    o_ref[...] = _dot(h_ref[...], w_ref[...]).astype(o_ref.dtype)


def _proj_sigmoid_kernel(h_ref, w_ref, o_ref):
    o_ref[...] = _sigmoid(_dot(h_ref[...], w_ref[...])).astype(o_ref.dtype)


def _proj_headnorm_kernel(h_ref, w_ref, g_ref, o_ref, *, group):
    acc = _dot(h_ref[...], w_ref[...])
    for c in range(acc.shape[1] // group):
        sl = slice(c * group, (c + 1) * group)
        blk = acc[:, sl]
        ms = jnp.mean(blk * blk, axis=-1, keepdims=True)
        o_ref[:, sl] = (blk * lax.rsqrt(ms + EPS) * g_ref[:, sl]).astype(o_ref.dtype)


def _causal_conv(raw_ref, cw_ref, cb_ref, tm, kw):
    raw = raw_ref[...]
    y = cb_ref[...] + cw_ref[kw - 1:kw, :] * raw[HALO:, :]
    for d in range(1, kw):
        y = y + cw_ref[kw - 1 - d:kw - d, :] * pltpu.roll(raw, d, axis=0)[HALO:, :]
    return y


def _shift_halo(raw_ref, tm, tiles_per_seq):
    i = pl.program_id(1)

    @pl.when(i % tiles_per_seq == 0)
    def _():
        raw_ref[0:HALO, :] = jnp.zeros((HALO, raw_ref.shape[1]), F32)

    @pl.when(i % tiles_per_seq != 0)
    def _():
        raw_ref[0:HALO, :] = raw_ref[tm:tm + HALO, :]


def _proj_conv_silu_kernel(h_ref, w_ref, cw_ref, cb_ref, ps_ref, o_ref, raw_ref, *, tm, kw, tiles_per_seq):
    _shift_halo(raw_ref, tm, tiles_per_seq)
    raw_ref[HALO:HALO + tm, :] = _dot(h_ref[...], w_ref[...])
    y = _causal_conv(raw_ref, cw_ref, cb_ref, tm, kw)
    o_ref[...] = (y * _sigmoid(y) * ps_ref[...]).astype(o_ref.dtype)


def _proj(h, w, kind, tm, tn, extra=(), seq=None):
    n, k = h.shape
    m = w.shape[1]
    grid = (m // tn, n // tm)
    h_spec = pl.BlockSpec((tm, k), lambda j, i: (i, 0))
    w_spec = pl.BlockSpec((k, tn), lambda j, i: (0, j))
    o_spec = pl.BlockSpec((tm, tn), lambda j, i: (i, j))
    row_spec = lambda r: pl.BlockSpec((r, tn), lambda j, i: (0, j))
    scratch = []
    sem = ("parallel", "parallel")
    if kind == "plain":
        body, in_specs = _proj_plain_kernel, [h_spec, w_spec]
    elif kind == "sigmoid":
        body, in_specs = _proj_sigmoid_kernel, [h_spec, w_spec]
    elif kind == "headnorm":
        body = functools.partial(_proj_headnorm_kernel, group=B_HEAD_DIM)
        in_specs = [h_spec, w_spec, row_spec(1)]
    elif kind == "conv_silu":
        kw = extra[0].shape[0]
        body = functools.partial(_proj_conv_silu_kernel, tm=tm, kw=kw, tiles_per_seq=seq // tm)
        in_specs = [h_spec, w_spec, row_spec(kw), row_spec(1), row_spec(1)]
        scratch = [pltpu.VMEM((tm + HALO, tn), F32)]
        sem = ("parallel", "arbitrary")
    else:
        raise ValueError(kind)
    return pl.pallas_call(
        body,
        grid=grid,
        in_specs=in_specs,
        out_specs=o_spec,
        out_shape=jax.ShapeDtypeStruct((n, m), BF16),
        scratch_shapes=scratch,
        compiler_params=_params(*sem),
        name="proj_" + kind,
    )(h, w, *extra)


def _log_sigmoid(x):
    return -(jnp.maximum(-x, 0.0) + jnp.log1p(jnp.exp(-jnp.abs(x))))


def _mlstm_kernel(qk_ref, v_ref, gc_ref, gr_ref, so_ref, sg_ref, ng_ref, o_ref, *state):
    L = CHUNK
    H = A_HEADS
    c_refs, n_refs, m_ref = state[:H], state[H:2 * H], state[2 * H]

    @pl.when(pl.program_id(1) == 0)
    def _():
        for r in state:
            r[...] = jnp.zeros(r.shape, F32)

    row = lax.broadcasted_iota(jnp.int32, (L, L), 0)
    col = lax.broadcasted_iota(jnp.int32, (L, L), 1)
    lower = (col <= row)
    tri_lower = lower.astype(F32)
    tri_upper = (row <= col).astype(F32)
    exact = functools.partial(jnp.dot, preferred_element_type=F32, precision=lax.Precision.HIGHEST)

    li_c = gc_ref[:, 0:H]
    b_c = exact(tri_lower, _log_sigmoid(gc_ref[:, H:2 * H]))
    m_st = m_ref[...]
    inter = b_c + m_st
    b_last = b_c[L - 1:L, :]
    ws = b_last - b_c + li_c
    m_new = jnp.maximum(b_last + m_st, jnp.max(ws, axis=0, keepdims=True))
    decay = jnp.exp(b_last + m_st - m_new)
    ws = jnp.exp(ws - m_new)
    m_ref[...] = m_new
    d_r = exact(_log_sigmoid(gr_ref[H:2 * H, :]), tri_upper) - gr_ref[0:H, :]

    hs = []
    for hd in range(H):
        q = qk_ref[:, hd * A_QK_DIM:(hd + 1) * A_QK_DIM]
        k = qk_ref[:, A_QK + hd * A_QK_DIM:A_QK + (hd + 1) * A_QK_DIM]
        v = v_ref[:, hd * A_V_DIM:(hd + 1) * A_V_DIM]
        c_st = c_refs[hd][...]
        n_st = n_refs[hd][...]
        inter_h = inter[:, hd:hd + 1]

        dmat = jnp.where(lower, b_c[:, hd:hd + 1] - d_r[hd:hd + 1, :], -jnp.inf)
        m_t = jnp.maximum(jnp.max(dmat, axis=-1, keepdims=True), inter_h)
        w_in = jnp.exp(dmat - m_t)
        g_inter = jnp.exp(inter_h - m_t)
        p = _dot_nt(q, k) * w_in
        num = _dot(p.astype(BF16), v) + g_inter * _dot(q, c_st.astype(BF16))
        den = (jnp.sum(p, axis=-1, keepdims=True)
               + g_inter * jnp.sum(q.astype(F32) * n_st, axis=-1, keepdims=True))
        hs.append(num / jnp.maximum(jnp.abs(den), jnp.exp(-m_t)))

        kw = k.astype(F32) * ws[:, hd:hd + 1]
        c_refs[hd][...] = decay[:, hd:hd + 1] * c_st + _dot(kw.T.astype(BF16), v)
        n_refs[hd][...] = decay[:, hd:hd + 1] * n_st + jnp.sum(kw, axis=0, keepdims=True)

    for hd in range(H):
        sl = slice(hd * A_V_DIM, (hd + 1) * A_V_DIM)
        hh = hs[hd]
        ms = jnp.mean(hh * hh, axis=-1, keepdims=True)
        y = hh * lax.rsqrt(ms + EPS) * ng_ref[:, sl]
        gate = so_ref[:, sl].astype(F32) * sg_ref[:, sl].astype(F32)
        o_ref[:, sl] = (gate * y).astype(o_ref.dtype)


def _mlstm(qk, vv, g_col, g_row, sig, norm_g, bsz, seq):
    n = qk.shape[0]
    nc = seq // CHUNK
    rows = lambda b, c: b * nc + c
    return pl.pallas_call(
        _mlstm_kernel,
        grid=(bsz, nc),
        in_specs=[pl.BlockSpec((CHUNK, 2 * A_QK), lambda b, c: (rows(b, c), 0)),
                  pl.BlockSpec((CHUNK, A_V), lambda b, c: (rows(b, c), 0)),
                  pl.BlockSpec((CHUNK, 2 * A_HEADS), lambda b, c: (rows(b, c), 0)),
                  pl.BlockSpec((2 * A_HEADS, CHUNK), lambda b, c: (0, rows(b, c))),
                  pl.BlockSpec((CHUNK, A_V), lambda b, c: (rows(b, c), 0)),
                  pl.BlockSpec((CHUNK, A_V), lambda b, c: (rows(b, c), 1)),
                  pl.BlockSpec((1, A_V), lambda b, c: (0, 0))],
        out_specs=pl.BlockSpec((CHUNK, A_V), lambda b, c: (rows(b, c), 0)),
        out_shape=jax.ShapeDtypeStruct((n, A_V), BF16),
        scratch_shapes=([pltpu.VMEM((A_QK_DIM, A_V_DIM), F32)] * A_HEADS
                        + [pltpu.VMEM((1, A_QK_DIM), F32)] * A_HEADS
                        + [pltpu.VMEM((1, A_HEADS), F32)]),
        compiler_params=_params("parallel", "arbitrary"),
        name="mlstm",
    )(qk, vv, g_col, g_row, sig, sig, norm_g.reshape(1, A_V))


def _diff_attn_kernel(slope_ref, q_ref, k_ref, v_ref, sg_ref, lp_ref, ng_ref, o_ref,
                      m_ref, l_ref, acc_ref, *, tq, tk, lam_init):
    dh = B_HEAD_DIM
    hw = 2 * dh
    hd = pl.program_id(1)
    i = pl.program_id(2)
    slope = slope_ref[hd]
    m_ref[...] = jnp.full(m_ref.shape, -jnp.inf, F32)
    l_ref[...] = jnp.zeros(l_ref.shape, F32)
    acc_ref[...] = jnp.zeros(acc_ref.shape, F32)
    key_pos = lax.broadcasted_iota(jnp.int32, (1, tk), 1).astype(F32)

    def step(j, masked):
        start = pl.multiple_of(j * tk, tk)
        kblk = k_ref[pl.ds(start, tk), :]
        vblk = v_ref[pl.ds(start, tk), :]
        bias = slope * (key_pos - ((i - j) * tk).astype(F32))
        ps, alphas = [], []
        for c in range(2):
            s = _dot_nt(q_ref[:, c * dh:(c + 1) * dh], kblk[:, c * dh:(c + 1) * dh]) + bias
            if masked:
                r = lax.broadcasted_iota(jnp.int32, (tq, tk), 0)
                cc = lax.broadcasted_iota(jnp.int32, (tq, tk), 1)
                s = jnp.where(cc <= r, s, -jnp.inf)
            m_old = m_ref[c]
            m_new = jnp.maximum(m_old, jnp.max(s, axis=-1, keepdims=True))
            alpha = jnp.exp2(m_old - m_new)
            p = jnp.exp2(s - _lane_tile(m_new, tk))
            l_ref[c] = alpha * l_ref[c] + jnp.sum(p, axis=-1, keepdims=True)
            m_ref[c] = m_new
            ps.append(p.astype(BF16))
            alphas.append(alpha)
        pv = _dot(jnp.concatenate(ps, axis=0), vblk)
        for c in range(2):
            acc_ref[c] = _lane_tile(alphas[c], hw) * acc_ref[c] + pv[c * tq:(c + 1) * tq]

    def body(j, carry):
        step(j, False)
        return carry

    lax.fori_loop(0, i, body, 0)
    step(i, True)

    lp = lp_ref[...]
    lam = (jnp.exp(jnp.sum(lp[0:1] * lp[1:2], axis=-1, keepdims=True))
           - jnp.exp(jnp.sum(lp[2:3] * lp[3:4], axis=-1, keepdims=True)) + lam_init)
    inv0 = 1.0 / l_ref[0]
    inv1 = lam / l_ref[1]
    o = acc_ref[0] * _lane_tile(inv0, hw) - acc_ref[1] * _lane_tile(inv1, hw)
    ms = jnp.mean(o * o, axis=-1, keepdims=True)
    y = o * lax.rsqrt(ms + EPS) * ng_ref[...] * (1.0 - lam_init)
    o_ref[...] = (sg_ref[...].astype(F32) * y).astype(o_ref.dtype)


def _diff_attn(qk, vv, sig, lam_params, subln_g, lam_init, bsz, seq, tq):
    n = qk.shape[0]
    hw = 2 * B_HEAD_DIM
    nq = seq // tq
    slopes = jnp.asarray(2.0 ** (-8.0 * np.arange(1, B_HEADS + 1) / B_HEADS) * LOG2E, dtype=F32)
    q_cols = B_QK // hw
    v_cols = A_V // hw
    g_cols = 2 * A_V // hw
    kernel = functools.partial(_diff_attn_kernel, tq=tq, tk=tq, lam_init=lam_init)
    grid_spec = pltpu.PrefetchScalarGridSpec(
        num_scalar_prefetch=1,
        grid=(bsz, B_HEADS, nq),
        in_specs=[pl.BlockSpec((tq, hw), lambda b, h, i, s: (b * nq + i, h)),
                  pl.BlockSpec((seq, hw), lambda b, h, i, s: (b, q_cols + h)),
                  pl.BlockSpec((seq, hw), lambda b, h, i, s: (b, v_cols + h)),
                  pl.BlockSpec((tq, hw), lambda b, h, i, s: (b * nq + i, g_cols + h)),
                  pl.BlockSpec((4, B_HEAD_DIM), lambda b, h, i, s: (0, 0)),
                  pl.BlockSpec((1, hw), lambda b, h, i, s: (0, 0))],
        out_specs=pl.BlockSpec((tq, hw), lambda b, h, i, s: (b * nq + i, h)),
        scratch_shapes=[pltpu.VMEM((2, tq, LANES), F32),
                        pltpu.VMEM((2, tq, LANES), F32),
                        pltpu.VMEM((2, tq, hw), F32)],
    )
    return pl.pallas_call(
        kernel,
        grid_spec=grid_spec,
        out_shape=jax.ShapeDtypeStruct((n, B_V), BF16),
        compiler_params=_params("parallel", "parallel", "parallel"),
        name="diff_attn",
    )(slopes, qk, qk, vv, sig, lam_params, subln_g.reshape(1, hw))


def _out_proj_kernel(ya_ref, yb_ref, w_ref, x_ref, g_ref, x1_ref, h2_ref):
    y = (ya_ref[...].astype(F32) + yb_ref[...].astype(F32)).astype(BF16)
    x1 = x_ref[...] + _dot(y, w_ref[...])
    x1_ref[...] = x1
    ms = jnp.mean(x1 * x1, axis=-1, keepdims=True)
    h2_ref[...] = (x1 * lax.rsqrt(ms + EPS) * g_ref[...]).astype(h2_ref.dtype)


def _out_proj(ya, yb, w, x, g, tm):
    n, d = x.shape
    row = pl.BlockSpec((tm, d), lambda i: (i, 0))
    return pl.pallas_call(
        _out_proj_kernel,
        grid=(n // tm,),
        in_specs=[row, row, pl.BlockSpec((d, d), lambda i: (0, 0)), row,
                  pl.BlockSpec((1, d), lambda i: (0, 0))],
        out_specs=[row, row],
        out_shape=[jax.ShapeDtypeStruct((n, d), F32), jax.ShapeDtypeStruct((n, d), BF16)],
        compiler_params=_params("parallel"),
        name="out_proj",
    )(ya, yb, w, x, g.reshape(1, d))


def _ffn_up_kernel(h_ref, wg_ref, wv_ref, cwg_ref, cwv_ref, cbg_ref, cbv_ref, o_ref,
                   rg_ref, rv_ref, *, tm, kw, tiles_per_seq):
    _shift_halo(rg_ref, tm, tiles_per_seq)
    _shift_halo(rv_ref, tm, tiles_per_seq)
    h = h_ref[...]
    rg_ref[HALO:HALO + tm, :] = _dot(h, wg_ref[...])
    rv_ref[HALO:HALO + tm, :] = _dot(h, wv_ref[...])
    ug = _causal_conv(rg_ref, cwg_ref, cbg_ref, tm, kw)
    uv = _causal_conv(rv_ref, cwv_ref, cbv_ref, tm, kw)
    o_ref[...] = (ug * _sigmoid(ug) * uv).astype(o_ref.dtype)


def _ffn_up(h, w_up, conv_w, conv_b, seq, tm, tn):
    n, k = h.shape
    nj = D_FF // tn
    kw = conv_w.shape[0]
    gate = lambda r: pl.BlockSpec((r, tn), lambda j, i: (0, j))
    val = lambda r: pl.BlockSpec((r, tn), lambda j, i: (0, nj + j))
    kernel = functools.partial(_ffn_up_kernel, tm=tm, kw=kw, tiles_per_seq=seq // tm)
    return pl.pallas_call(
        kernel,
        grid=(nj, n // tm),
        in_specs=[pl.BlockSpec((tm, k), lambda j, i: (i, 0)),
                  gate(k), val(k), gate(kw), val(kw), gate(1), val(1)],
        out_specs=pl.BlockSpec((tm, tn), lambda j, i: (i, j)),
        out_shape=jax.ShapeDtypeStruct((n, D_FF), BF16),
        scratch_shapes=[pltpu.VMEM((tm + HALO, tn), F32), pltpu.VMEM((tm + HALO, tn), F32)],
        compiler_params=_params("parallel", "arbitrary"),
        name="ffn_up",
    )(h, w_up, w_up, conv_w, conv_w, conv_b, conv_b)


def _ffn_down_kernel(a_ref, w_ref, x_ref, o_ref):
    k = pl.program_id(1)

    @pl.when(k == 0)
    def _():
        o_ref[...] = x_ref[...]

    o_ref[...] += _dot(a_ref[...], w_ref[...])


def _ffn_down(act, w, x1, tm, tk):
    n, d = x1.shape
    return pl.pallas_call(
        _ffn_down_kernel,
        grid=(n // tm, D_FF // tk),
        in_specs=[pl.BlockSpec((tm, tk), lambda i, k: (i, k)),
                  pl.BlockSpec((tk, d), lambda i, k: (k, 0)),
                  pl.BlockSpec((tm, d), lambda i, k: (i, 0))],
        out_specs=pl.BlockSpec((tm, d), lambda i, k: (i, 0)),
        out_shape=jax.ShapeDtypeStruct((n, d), F32),
        compiler_params=_params("parallel", "arbitrary"),
        name="ffn_down",
    )(act, w, x1)


def _layer(x2, bsz, seq, l, norm1_g, w_in, if_bias, qk_conv_w, qk_conv_b, mlstm_norm_g, q_norm_g,
           k_norm_g, diff_lambda, subln_g, w_out, norm2_g, w_up, ffn_conv_w, ffn_conv_b, w_down):
    n = bsz * seq
    tm = min(1024, seq)
    off = np.cumsum((0,) + SPLIT_SIZES)
    cols = lambda g: w_in[:, off[g]:off[g + 1]]
    w_aqk = cols(0).astype(BF16)
    w_plain = jnp.concatenate([cols(1), cols(6)], axis=1).astype(BF16)
    w_sig = jnp.concatenate([cols(2), cols(7), cols(8)], axis=1).astype(BF16)
    w_if = cols(3).astype(BF16)
    w_bqk = jnp.concatenate([cols(4), cols(5)], axis=1).astype(BF16)

    h, g_col, g_row = _rmsnorm_gates(x2, norm1_g, w_if, if_bias, min(512, seq))

    q_scale = jnp.concatenate([jnp.full((A_QK,), A_QK_DIM ** -0.5, F32), jnp.ones((A_QK,), F32)])
    qk_m = _proj(h, w_aqk, "conv_silu", tm, 512,
                 extra=(qk_conv_w, qk_conv_b.reshape(1, -1), q_scale.reshape(1, -1)), seq=seq)
    vv = _proj(h, w_plain, "plain", tm, 2048)
    sig = _proj(h, w_sig, "sigmoid", tm, 2048)
    reps = B_QK // B_HEAD_DIM
    qk_gain = jnp.concatenate([jnp.tile(q_norm_g * (B_HEAD_DIM ** -0.5 * LOG2E), reps), jnp.tile(k_norm_g, reps)])
    qk_d = _proj(h, w_bqk, "headnorm", tm, 2048, extra=(qk_gain.reshape(1, -1),))

    y_a = _mlstm(qk_m, vv, g_col, g_row, sig, mlstm_norm_g, bsz, seq)
    lam_init = 0.8 - 0.6 * math.exp(-0.3 * l)
    y_b = _diff_attn(qk_d, vv, sig, diff_lambda, subln_g, lam_init, bsz, seq, min(512, seq))

    x1, h2 = _out_proj(y_a, y_b, w_out.astype(BF16), x2, norm2_g, min(512, seq))
    act = _ffn_up(h2, w_up.astype(BF16), ffn_conv_w, ffn_conv_b.reshape(1, -1), seq, tm, 512)
    return _ffn_down(act, w_down.astype(BF16), x1, tm, D_FF // 4)


def kernel(x, norm1_g, w_in, if_bias, qk_conv_w, qk_conv_b, mlstm_norm_g, q_norm_g, k_norm_g,
           diff_lambda, subln_g, w_out, norm2_g, w_up, ffn_conv_w, ffn_conv_b, w_down):
    bsz, seq, d = x.shape
    x2 = x.reshape(bsz * seq, d)
    for l in range(norm1_g.shape[0]):
        x2 = _layer(x2, bsz, seq, l, norm1_g[l], w_in[l], if_bias[l], qk_conv_w[l], qk_conv_b[l],
                    mlstm_norm_g[l], q_norm_g[l], k_norm_g[l], diff_lambda[l], subln_g[l], w_out[l],
                    norm2_g[l], w_up[l], ffn_conv_w[l], ffn_conv_b[l], w_down[l])
    return x2.reshape(bsz, seq, d)
```

```python
import functools
import math

import numpy as np
import jax
import jax.numpy as jnp
from jax import lax
from jax.experimental import pallas as pl
from jax.experimental.pallas import tpu as pltpu

F32 = jnp.float32
BF16 = jnp.bfloat16

D_MODEL = 2048
A_HEADS = 4
A_QK_DIM = 256
A_V_DIM = 512
A_CONV = 4
CHUNK = 128
B_HEADS = 8
B_HEAD_DIM = 128
D_FF = 5632
FFN_CONV = 3
EPS = 1e-6

A_QK = A_HEADS * A_QK_DIM
A_V = A_HEADS * A_V_DIM
B_QK = B_HEADS * 2 * B_HEAD_DIM
B_V = B_HEADS * 2 * B_HEAD_DIM
SPLIT_SIZES = (2 * A_QK, A_V, A_V, 2 * A_HEADS, B_QK, B_QK, B_V, D_MODEL, D_MODEL)

SUBLANES = 8
LANES = 128
LOG2E = math.log2(math.e)
HALO = SUBLANES
VMEM_LIMIT = 56 * 1024 * 1024


def _params(*sem):
    return pltpu.CompilerParams(dimension_semantics=sem, vmem_limit_bytes=VMEM_LIMIT)


def _dot(a, b):
    return jnp.dot(a, b, preferred_element_type=F32)


def _dot_nt(a, b):
    return lax.dot_general(a, b, (((1,), (1,)), ((), ())), preferred_element_type=F32)


def _sigmoid(x):
    return 1.0 / (1.0 + jnp.exp(-x))


def _lane_tile(x, width):
    return jnp.concatenate([x] * (width // LANES), axis=1)


def _rmsnorm_gates_kernel(x_ref, g_ref, w_ref, wt_ref, bc_ref, br_ref, o_ref, oc_ref, or_ref):
    x = x_ref[...]
    ms = jnp.mean(x * x, axis=-1, keepdims=True)
    h = (x * lax.rsqrt(ms + EPS) * g_ref[...]).astype(o_ref.dtype)
    o_ref[...] = h
    oc_ref[...] = _dot(h, w_ref[...]) + bc_ref[...]
    or_ref[...] = _dot_nt(wt_ref[...], h) + br_ref[...]


def _rmsnorm_gates(x, g, w_if, bias, tm):
    n, d = x.shape
    ng = w_if.shape[1]
    const = lambda r, c: pl.BlockSpec((r, c), lambda i: (0, 0))
    return pl.pallas_call(
        _rmsnorm_gates_kernel,
        grid=(n // tm,),
        in_specs=[pl.BlockSpec((tm, d), lambda i: (i, 0)), const(1, d),
                  const(d, ng), const(ng, d), const(1, ng), const(ng, 1)],
        out_specs=[pl.BlockSpec((tm, d), lambda i: (i, 0)),
                   pl.BlockSpec((tm, ng), lambda i: (i, 0)),
                   pl.BlockSpec((ng, tm), lambda i: (0, i))],
        out_shape=[jax.ShapeDtypeStruct((n, d), BF16),
                   jax.ShapeDtypeStruct((n, ng), F32), jax.ShapeDtypeStruct((ng, n), F32)],
        compiler_params=_params("parallel"),
        name="rmsnorm_gates",
    )(x, g.reshape(1, d), w_if, w_if.T, bias.reshape(1, ng), bias.reshape(ng, 1))


def _proj_plain_kernel(h_ref, w_ref, o_ref):
    o_ref[...] = _dot(h_ref[...], w_ref[...]).astype(o_ref.dtype)


def _proj_sigmoid_kernel(h_ref, w_ref, o_ref):
    o_ref[...] = _sigmoid(_dot(h_ref[...], w_ref[...])).astype(o_ref.dtype)


def _proj_headnorm_kernel(h_ref, w_ref, g_ref, o_ref, *, group):
    acc = _dot(h_ref[...], w_ref[...])
    for c in range(acc.shape[1] // group):
        sl = slice(c * group, (c + 1) * group)
        blk = acc[:, sl]
        ms = jnp.mean(blk * blk, axis=-1, keepdims=True)
        o_ref[:, sl] = (blk * lax.rsqrt(ms + EPS) * g_ref[:, sl]).astype(o_ref.dtype)


def _causal_conv(raw_ref, cw_ref, cb_ref, tm, kw):
    raw = raw_ref[...]
    y = cb_ref[...] + cw_ref[kw - 1:kw, :] * raw[HALO:, :]
    for d in range(1, kw):
        y = y + cw_ref[kw - 1 - d:kw - d, :] * pltpu.roll(raw, d, axis=0)[HALO:, :]
    return y


def _shift_halo(raw_ref, tm, tiles_per_seq):
    i = pl.program_id(1)

    @pl.when(i % tiles_per_seq == 0)
    def _():
        raw_ref[0:HALO, :] = jnp.zeros((HALO, raw_ref.shape[1]), F32)

    @pl.when(i % tiles_per_seq != 0)
    def _():
        raw_ref[0:HALO, :] = raw_ref[tm:tm + HALO, :]


def _proj_conv_silu_kernel(h_ref, w_ref, cw_ref, cb_ref, ps_ref, o_ref, raw_ref, *, tm, kw, tiles_per_seq):
    _shift_halo(raw_ref, tm, tiles_per_seq)
    raw_ref[HALO:HALO + tm, :] = _dot(h_ref[...], w_ref[...])
    y = _causal_conv(raw_ref, cw_ref, cb_ref, tm, kw)
    o_ref[...] = (y * _sigmoid(y) * ps_ref[...]).astype(o_ref.dtype)


def _proj(h, w, kind, tm, tn, extra=(), seq=None):
    n, k = h.shape
    m = w.shape[1]
    grid = (m // tn, n // tm)
    h_spec = pl.BlockSpec((tm, k), lambda j, i: (i, 0))
    w_spec = pl.BlockSpec((k, tn), lambda j, i: (0, j))
    o_spec = pl.BlockSpec((tm, tn), lambda j, i: (i, j))
    row_spec = lambda r: pl.BlockSpec((r, tn), lambda j, i: (0, j))
    scratch = []
    sem = ("parallel", "parallel")
    if kind == "plain":
        body, in_specs = _proj_plain_kernel, [h_spec, w_spec]
    elif kind == "sigmoid":
        body, in_specs = _proj_sigmoid_kernel, [h_spec, w_spec]
    elif kind == "headnorm":
        body = functools.partial(_proj_headnorm_kernel, group=B_HEAD_DIM)
        in_specs = [h_spec, w_spec, row_spec(1)]
    elif kind == "conv_silu":
        kw = extra[0].shape[0]
        body = functools.partial(_proj_conv_silu_kernel, tm=tm, kw=kw, tiles_per_seq=seq // tm)
        in_specs = [h_spec, w_spec, row_spec(kw), row_spec(1), row_spec(1)]
        scratch = [pltpu.VMEM((tm + HALO, tn), F32)]
        sem = ("parallel", "arbitrary")
    else:
        raise ValueError(kind)
    return pl.pallas_call(
        body,
        grid=grid,
        in_specs=in_specs,
        out_specs=o_spec,
        out_shape=jax.ShapeDtypeStruct((n, m), BF16),
        scratch_shapes=scratch,
        compiler_params=_params(*sem),
        name="proj_" + kind,
    )(h, w, *extra)


def _log_sigmoid(x):
    return -(jnp.maximum(-x, 0.0) + jnp.log1p(jnp.exp(-jnp.abs(x))))


def _mlstm_kernel(qk_ref, v_ref, gc_ref, gr_ref, so_ref, sg_ref, ng_ref, o_ref, *state):
    L = CHUNK
    H = A_HEADS
    c_refs, n_refs, m_ref = state[:H], state[H:2 * H], state[2 * H]

    @pl.when(pl.program_id(1) == 0)
    def _():
        for r in state:
            r[...] = jnp.zeros(r.shape, F32)

    row = lax.broadcasted_iota(jnp.int32, (L, L), 0)
    col = lax.broadcasted_iota(jnp.int32, (L, L), 1)
    lower = (col <= row)
    tri_lower = lower.astype(F32)
    tri_upper = (row <= col).astype(F32)
    exact = functools.partial(jnp.dot, preferred_element_type=F32, precision=lax.Precision.HIGHEST)

    li_c = gc_ref[:, 0:H]
    b_c = exact(tri_lower, _log_sigmoid(gc_ref[:, H:2 * H]))
    m_st = m_ref[...]
    inter = b_c + m_st
    b_last = b_c[L - 1:L, :]
    ws = b_last - b_c + li_c
    m_new = jnp.maximum(b_last + m_st, jnp.max(ws, axis=0, keepdims=True))
    decay = jnp.exp(b_last + m_st - m_new)
    ws = jnp.exp(ws - m_new)
    m_ref[...] = m_new
    d_r = exact(_log_sigmoid(gr_ref[H:2 * H, :]), tri_upper) - gr_ref[0:H, :]

    hs = []
    for hd in range(H):
        q = qk_ref[:, hd * A_QK_DIM:(hd + 1) * A_QK_DIM]
        k = qk_ref[:, A_QK + hd * A_QK_DIM:A_QK + (hd + 1) * A_QK_DIM]
        v = v_ref[:, hd * A_V_DIM:(hd + 1) * A_V_DIM]
        c_st = c_refs[hd][...]
        n_st = n_refs[hd][...]
        inter_h = inter[:, hd:hd + 1]

        dmat = jnp.where(lower, b_c[:, hd:hd + 1] - d_r[hd:hd + 1, :], -jnp.inf)
        m_t = jnp.maximum(jnp.max(dmat, axis=-1, keepdims=True), inter_h)
        w_in = jnp.exp(dmat - m_t)
        g_inter = jnp.exp(inter_h - m_t)
        p = _dot_nt(q, k) * w_in
        num = _dot(p.astype(BF16), v) + g_inter * _dot(q, c_st.astype(BF16))
        den = (jnp.sum(p, axis=-1, keepdims=True)
               + g_inter * jnp.sum(q.astype(F32) * n_st, axis=-1, keepdims=True))
        hs.append(num / jnp.maximum(jnp.abs(den), jnp.exp(-m_t)))

        kw = k.astype(F32) * ws[:, hd:hd + 1]
        c_refs[hd][...] = decay[:, hd:hd + 1] * c_st + _dot(kw.T.astype(BF16), v)
        n_refs[hd][...] = decay[:, hd:hd + 1] * n_st + jnp.sum(kw, axis=0, keepdims=True)

    for hd in range(H):
        sl = slice(hd * A_V_DIM, (hd + 1) * A_V_DIM)
        hh = hs[hd]
        ms = jnp.mean(hh * hh, axis=-1, keepdims=True)
        y = hh * lax.rsqrt(ms + EPS) * ng_ref[:, sl]
        gate = so_ref[:, sl].astype(F32) * sg_ref[:, sl].astype(F32)
        o_ref[:, sl] = (gate * y).astype(o_ref.dtype)


def _mlstm(qk, vv, g_col, g_row, sig, norm_g, bsz, seq):
    n = qk.shape[0]
    nc = seq // CHUNK
    rows = lambda b, c: b * nc + c
    return pl.pallas_call(
        _mlstm_kernel,
        grid=(bsz, nc),
        in_specs=[pl.BlockSpec((CHUNK, 2 * A_QK), lambda b, c: (rows(b, c), 0)),
                  pl.BlockSpec((CHUNK, A_V), lambda b, c: (rows(b, c), 0)),
                  pl.BlockSpec((CHUNK, 2 * A_HEADS), lambda b, c: (rows(b, c), 0)),
                  pl.BlockSpec((2 * A_HEADS, CHUNK), lambda b, c: (0, rows(b, c))),
                  pl.BlockSpec((CHUNK, A_V), lambda b, c: (rows(b, c), 0)),
                  pl.BlockSpec((CHUNK, A_V), lambda b, c: (rows(b, c), 1)),
                  pl.BlockSpec((1, A_V), lambda b, c: (0, 0))],
        out_specs=pl.BlockSpec((CHUNK, A_V), lambda b, c: (rows(b, c), 0)),
        out_shape=jax.ShapeDtypeStruct((n, A_V), BF16),
        scratch_shapes=([pltpu.VMEM((A_QK_DIM, A_V_DIM), F32)] * A_HEADS
                        + [pltpu.VMEM((1, A_QK_DIM), F32)] * A_HEADS
                        + [pltpu.VMEM((1, A_HEADS), F32)]),
        compiler_params=_params("parallel", "arbitrary"),
        name="mlstm",
    )(qk, vv, g_col, g_row, sig, sig, norm_g.reshape(1, A_V))


def _diff_attn_kernel(slope_ref, q_ref, k_ref, v_ref, sg_ref, lp_ref, ng_ref, o_ref,
                      m_ref, l_ref, acc_ref, *, tq, tk, lam_init):
    dh = B_HEAD_DIM
    hw = 2 * dh
    hd = pl.program_id(1)
    i = pl.program_id(2)
    slope = slope_ref[hd]
    m_ref[...] = jnp.full(m_ref.shape, -jnp.inf, F32)
    l_ref[...] = jnp.zeros(l_ref.shape, F32)
    acc_ref[...] = jnp.zeros(acc_ref.shape, F32)
    key_pos = lax.broadcasted_iota(jnp.int32, (1, tk), 1).astype(F32)

    def step(j, masked):
        start = pl.multiple_of(j * tk, tk)
        kblk = k_ref[pl.ds(start, tk), :]
        vblk = v_ref[pl.ds(start, tk), :]
        bias = slope * (key_pos - ((i - j) * tk).astype(F32))
        ps, alphas = [], []
        for c in range(2):
            s = _dot_nt(q_ref[:, c * dh:(c + 1) * dh], kblk[:, c * dh:(c + 1) * dh]) + bias
            if masked:
                r = lax.broadcasted_iota(jnp.int32, (tq, tk), 0)
                cc = lax.broadcasted_iota(jnp.int32, (tq, tk), 1)
                s = jnp.where(cc <= r, s, -jnp.inf)
            m_old = m_ref[c]
            m_new = jnp.maximum(m_old, jnp.max(s, axis=-1, keepdims=True))
            alpha = jnp.exp2(m_old - m_new)
            p = jnp.exp2(s - _lane_tile(m_new, tk))
            l_ref[c] = alpha * l_ref[c] + jnp.sum(p, axis=-1, keepdims=True)
            m_ref[c] = m_new
            ps.append(p.astype(BF16))
            alphas.append(alpha)
        pv = _dot(jnp.concatenate(ps, axis=0), vblk)
        for c in range(2):
            acc_ref[c] = _lane_tile(alphas[c], hw) * acc_ref[c] + pv[c * tq:(c + 1) * tq]

    def body(j, carry):
        step(j, False)
        return carry

    lax.fori_loop(0, i, body, 0)
    step(i, True)

    _diff_attn_finish(acc_ref, 1.0 / l_ref[0], 1.0 / l_ref[1], sg_ref, lp_ref, ng_ref, o_ref, lam_init)


def _diff_attn_finish(acc_ref, inv0, inv1, sg_ref, lp_ref, ng_ref, o_ref, lam_init):
    hw = acc_ref.shape[-1]
    lp = lp_ref[...]
    lam = (jnp.exp(jnp.sum(lp[0:1] * lp[1:2], axis=-1, keepdims=True))
           - jnp.exp(jnp.sum(lp[2:3] * lp[3:4], axis=-1, keepdims=True)) + lam_init)
    o = acc_ref[0] * _lane_tile(inv0, hw) - acc_ref[1] * _lane_tile(lam * inv1, hw)
    ms = jnp.mean(o * o, axis=-1, keepdims=True)
    y = o * lax.rsqrt(ms + EPS) * ng_ref[...] * (1.0 - lam_init)
    o_ref[...] = (sg_ref[...].astype(F32) * y).astype(o_ref.dtype)


FIXED_OFFSET_MAX = 48.0
SPLIT_PIECES = 3
KEY_POS_RADIX = 64


def _split_bf16(x):
    pieces = []
    for _ in range(SPLIT_PIECES):
        p = x.astype(BF16).astype(F32)
        pieces.append(p)
        x = x - p
    return pieces


def _key_features(seq):
    s = np.arange(seq)
    f = np.zeros((seq, LANES), np.float32)
    for t in range(SPLIT_PIECES):
        f[:, 2 * t] = (s // KEY_POS_RADIX) * KEY_POS_RADIX
        f[:, 2 * t + 1] = s % KEY_POS_RADIX
    f[:, 2 * SPLIT_PIECES:3 * SPLIT_PIECES] = 1.0
    return jnp.asarray(f, dtype=BF16)


def _diff_attn_fixed_kernel(tab_ref, q_ref, k_ref, kf_ref, v_ref, sg_ref, lp_ref, ng_ref, o_ref,
                            l_ref, acc_ref, qa_ref, *, tq, tk, lam_init):
    dh = B_HEAD_DIM
    hd = pl.program_id(1)
    i = pl.program_id(2)
    slope = tab_ref[hd]
    offset = tab_ref[B_HEADS]
    l_ref[...] = jnp.zeros(l_ref.shape, F32)
    acc_ref[...] = jnp.zeros(acc_ref.shape, F32)

    lane = lax.broadcasted_iota(jnp.int32, (tq, LANES), 1)
    t_pos = (lax.broadcasted_iota(jnp.int32, (tq, LANES), 0) + i * tq).astype(F32)
    slope_parts = _split_bf16(jnp.full((tq, LANES), slope, F32))
    row_parts = _split_bf16(-(slope * t_pos) - offset)
    feat = jnp.zeros((tq, LANES), F32)
    for t in range(SPLIT_PIECES):
        feat = jnp.where(jnp.right_shift(lane, 1) == t, slope_parts[t], feat)
        feat = jnp.where(lane == 2 * SPLIT_PIECES + t, row_parts[t], feat)
    feat = feat.astype(BF16)
    for c in range(2):
        qa_ref[c] = jnp.concatenate([q_ref[:, c * dh:(c + 1) * dh], feat], axis=1)

    def step(j, masked):
        start = pl.multiple_of(j * tk, tk)
        kblk = k_ref[pl.ds(start, tk), :]
        kfblk = kf_ref[pl.ds(start, tk), :]
        vblk = v_ref[pl.ds(start, tk), :]
        ps = []
        for c in range(2):
            ka = jnp.concatenate([kblk[:, c * dh:(c + 1) * dh], kfblk], axis=1)
            s = _dot_nt(qa_ref[c], ka)
            if masked:
                r = lax.broadcasted_iota(jnp.int32, (tq, tk), 0)
                cc = lax.broadcasted_iota(jnp.int32, (tq, tk), 1)
                s = jnp.where(cc <= r, s, -jnp.inf)
            p = jnp.exp2(s)
            part = p[:, 0:LANES]
            for g in range(1, tk // LANES):
                part = part + p[:, g * LANES:(g + 1) * LANES]
            l_ref[c] += part
            ps.append(p.astype(BF16))
        pv = _dot(jnp.concatenate(ps, axis=0), vblk)
        for c in range(2):
            acc_ref[c] += pv[c * tq:(c + 1) * tq]

    def body(j, carry):
        step(j, False)
        return carry

    lax.fori_loop(0, i, body, 0)
    step(i, True)

    inv = [jnp.broadcast_to(1.0 / jnp.sum(l_ref[c], axis=-1, keepdims=True), (tq, LANES)) for c in range(2)]
    _diff_attn_finish(acc_ref, inv[0], inv[1], sg_ref, lp_ref, ng_ref, o_ref, lam_init)


def _diff_attn(qk, vv, sig, lam_params, subln_g, lam_init, bsz, seq, tq, offset=None):
    n = qk.shape[0]
    hw = 2 * B_HEAD_DIM
    nq = seq // tq
    slopes = jnp.asarray(2.0 ** (-8.0 * np.arange(1, B_HEADS + 1) / B_HEADS) * LOG2E, dtype=F32)
    q_cols = B_QK // hw
    v_cols = A_V // hw
    g_cols = 2 * A_V // hw
    q_spec = pl.BlockSpec((tq, hw), lambda b, h, i, s: (b * nq + i, h))
    k_spec = pl.BlockSpec((seq, hw), lambda b, h, i, s: (b, q_cols + h))
    tail_specs = [pl.BlockSpec((seq, hw), lambda b, h, i, s: (b, v_cols + h)),
                  pl.BlockSpec((tq, hw), lambda b, h, i, s: (b * nq + i, g_cols + h)),
                  pl.BlockSpec((4, B_HEAD_DIM), lambda b, h, i, s: (0, 0)),
                  pl.BlockSpec((1, hw), lambda b, h, i, s: (0, 0))]
    tail_args = (vv, sig, lam_params, subln_g.reshape(1, hw))
    stats = pltpu.VMEM((2, tq, LANES), F32)
    acc = pltpu.VMEM((2, tq, hw), F32)
    if offset is None:
        kernel = functools.partial(_diff_attn_kernel, tq=tq, tk=tq, lam_init=lam_init)
        table, in_specs, args = slopes, [q_spec, k_spec] + tail_specs, (qk, qk) + tail_args
        scratch = [stats, stats, acc]
    else:
        kernel = functools.partial(_diff_attn_fixed_kernel, tq=tq, tk=tq, lam_init=lam_init)
        table = jnp.concatenate([slopes, jnp.reshape(offset, (1,)).astype(F32)])
        kf_spec = pl.BlockSpec((seq, LANES), lambda b, h, i, s: (0, 0))
        in_specs, args = [q_spec, k_spec, kf_spec] + tail_specs, (qk, qk, _key_features(seq)) + tail_args
        scratch = [stats, acc, pltpu.VMEM((2, tq, hw), BF16)]
    grid_spec = pltpu.PrefetchScalarGridSpec(
        num_scalar_prefetch=1,
        grid=(bsz, B_HEADS, nq),
        in_specs=in_specs,
        out_specs=pl.BlockSpec((tq, hw), lambda b, h, i, s: (b * nq + i, h)),
        scratch_shapes=scratch,
    )
    return pl.pallas_call(
        kernel,
        grid_spec=grid_spec,
        out_shape=jax.ShapeDtypeStruct((n, B_V), BF16),
        compiler_params=_params("parallel", "parallel", "parallel"),
        name="diff_attn" if offset is None else "diff_attn_fixed",
    )(table, *args)


def _out_proj_kernel(ya_ref, yb_ref, w_ref, x_ref, g_ref, x1_ref, h2_ref):
    y = (ya_ref[...].astype(F32) + yb_ref[...].astype(F32)).astype(BF16)
    x1 = x_ref[...] + _dot(y, w_ref[...])
    x1_ref[...] = x1
    ms = jnp.mean(x1 * x1, axis=-1, keepdims=True)
    h2_ref[...] = (x1 * lax.rsqrt(ms + EPS) * g_ref[...]).astype(h2_ref.dtype)


def _out_proj(ya, yb, w, x, g, tm):
    n, d = x.shape
    row = pl.BlockSpec((tm, d), lambda i: (i, 0))
    return pl.pallas_call(
        _out_proj_kernel,
        grid=(n // tm,),
        in_specs=[row, row, pl.BlockSpec((d, d), lambda i: (0, 0)), row,
                  pl.BlockSpec((1, d), lambda i: (0, 0))],
        out_specs=[row, row],
        out_shape=[jax.ShapeDtypeStruct((n, d), F32), jax.ShapeDtypeStruct((n, d), BF16)],
        compiler_params=_params("parallel"),
        name="out_proj",
    )(ya, yb, w, x, g.reshape(1, d))


def _ffn_up_kernel(h_ref, wg_ref, wv_ref, cwg_ref, cwv_ref, cbg_ref, cbv_ref, o_ref,
                   rg_ref, rv_ref, *, tm, kw, tiles_per_seq):
    _shift_halo(rg_ref, tm, tiles_per_seq)
    _shift_halo(rv_ref, tm, tiles_per_seq)
    h = h_ref[...]
    rg_ref[HALO:HALO + tm, :] = _dot(h, wg_ref[...])
    rv_ref[HALO:HALO + tm, :] = _dot(h, wv_ref[...])
    ug = _causal_conv(rg_ref, cwg_ref, cbg_ref, tm, kw)
    uv = _causal_conv(rv_ref, cwv_ref, cbv_ref, tm, kw)
    o_ref[...] = (ug * _sigmoid(ug) * uv).astype(o_ref.dtype)


def _ffn_up(h, w_up, conv_w, conv_b, seq, tm, tn):
    n, k = h.shape
    nj = D_FF // tn
    kw = conv_w.shape[0]
    gate = lambda r: pl.BlockSpec((r, tn), lambda j, i: (0, j))
    val = lambda r: pl.BlockSpec((r, tn), lambda j, i: (0, nj + j))
    kernel = functools.partial(_ffn_up_kernel, tm=tm, kw=kw, tiles_per_seq=seq // tm)
    return pl.pallas_call(
        kernel,
        grid=(nj, n // tm),
        in_specs=[pl.BlockSpec((tm, k), lambda j, i: (i, 0)),
                  gate(k), val(k), gate(kw), val(kw), gate(1), val(1)],
        out_specs=pl.BlockSpec((tm, tn), lambda j, i: (i, j)),
        out_shape=jax.ShapeDtypeStruct((n, D_FF), BF16),
        scratch_shapes=[pltpu.VMEM((tm + HALO, tn), F32), pltpu.VMEM((tm + HALO, tn), F32)],
        compiler_params=_params("parallel", "arbitrary"),
        name="ffn_up",
    )(h, w_up, w_up, conv_w, conv_w, conv_b, conv_b)


def _ffn_down_kernel(a_ref, w_ref, x_ref, o_ref):
    k = pl.program_id(1)

    @pl.when(k == 0)
    def _():
        o_ref[...] = x_ref[...]

    o_ref[...] += _dot(a_ref[...], w_ref[...])


def _ffn_down(act, w, x1, tm, tk):
    n, d = x1.shape
    return pl.pallas_call(
        _ffn_down_kernel,
        grid=(n // tm, D_FF // tk),
        in_specs=[pl.BlockSpec((tm, tk), lambda i, k: (i, k)),
                  pl.BlockSpec((tk, d), lambda i, k: (k, 0)),
                  pl.BlockSpec((tm, d), lambda i, k: (i, 0))],
        out_specs=pl.BlockSpec((tm, d), lambda i, k: (i, 0)),
        out_shape=jax.ShapeDtypeStruct((n, d), F32),
        compiler_params=_params("parallel", "arbitrary"),
        name="ffn_down",
    )(act, w, x1)


def _layer(x2, bsz, seq, l, norm1_g, w_in, if_bias, qk_conv_w, qk_conv_b, mlstm_norm_g, q_norm_g,
           k_norm_g, diff_lambda, subln_g, w_out, norm2_g, w_up, ffn_conv_w, ffn_conv_b, w_down):
    n = bsz * seq
    tm = min(1024, seq)
    off = np.cumsum((0,) + SPLIT_SIZES)
    cols = lambda g: w_in[:, off[g]:off[g + 1]]
    w_aqk = cols(0).astype(BF16)
    w_plain = jnp.concatenate([cols(1), cols(6)], axis=1).astype(BF16)
    w_sig = jnp.concatenate([cols(2), cols(7), cols(8)], axis=1).astype(BF16)
    w_if = cols(3).astype(BF16)
    w_bqk = jnp.concatenate([cols(4), cols(5)], axis=1).astype(BF16)

    h, g_col, g_row = _rmsnorm_gates(x2, norm1_g, w_if, if_bias, min(512, seq))

    q_scale = jnp.concatenate([jnp.full((A_QK,), A_QK_DIM ** -0.5, F32), jnp.ones((A_QK,), F32)])
    qk_m = _proj(h, w_aqk, "conv_silu", tm, 512,
                 extra=(qk_conv_w, qk_conv_b.reshape(1, -1), q_scale.reshape(1, -1)), seq=seq)
    vv = _proj(h, w_plain, "plain", tm, 2048)
    sig = _proj(h, w_sig, "sigmoid", tm, 2048)
    reps = B_QK // B_HEAD_DIM
    qk_gain = jnp.concatenate([jnp.tile(q_norm_g * (B_HEAD_DIM ** -0.5 * LOG2E), reps), jnp.tile(k_norm_g, reps)])
    qk_d = _proj(h, w_bqk, "headnorm", tm, 2048, extra=(qk_gain.reshape(1, -1),))

    y_a = _mlstm(qk_m, vv, g_col, g_row, sig, mlstm_norm_g, bsz, seq)
    lam_init = 0.8 - 0.6 * math.exp(-0.3 * l)
    logit_bound = 1.01 * B_HEAD_DIM ** 0.5 * LOG2E * jnp.max(jnp.abs(q_norm_g * k_norm_g))
    attn = functools.partial(_diff_attn, qk_d, vv, sig, diff_lambda, subln_g, lam_init, bsz, seq)
    y_b = lax.cond(logit_bound <= FIXED_OFFSET_MAX,
                   lambda: attn(min(1024, seq), logit_bound),
                   lambda: attn(min(512, seq)))

    x1, h2 = _out_proj(y_a, y_b, w_out.astype(BF16), x2, norm2_g, min(512, seq))
    act = _ffn_up(h2, w_up.astype(BF16), ffn_conv_w, ffn_conv_b.reshape(1, -1), seq, tm, 512)
    return _ffn_down(act, w_down.astype(BF16), x1, tm, D_FF // 4)


def kernel(x, norm1_g, w_in, if_bias, qk_conv_w, qk_conv_b, mlstm_norm_g, q_norm_g, k_norm_g,
           diff_lambda, subln_g, w_out, norm2_g, w_up, ffn_conv_w, ffn_conv_b, w_down):
    bsz, seq, d = x.shape
    x2 = x.reshape(bsz * seq, d)
    for l in range(norm1_g.shape[0]):
        x2 = _layer(x2, bsz, seq, l, norm1_g[l], w_in[l], if_bias[l], qk_conv_w[l], qk_conv_b[l],
                    mlstm_norm_g[l], q_norm_g[l], k_norm_g[l], diff_lambda[l], subln_g[l], w_out[l],
                    norm2_g[l], w_up[l], ffn_conv_w[l], ffn_conv_b[l], w_down[l])
    return x2.reshape(bsz, seq, d)
```

```python
import functools
import math

import numpy as np
import jax
import jax.numpy as jnp
from jax import lax
from jax.experimental import pallas as pl
from jax.experimental.pallas import tpu as pltpu

F32 = jnp.float32
BF16 = jnp.bfloat16

D_MODEL = 2048
A_HEADS = 4
A_QK_DIM = 256
A_V_DIM = 512
A_CONV = 4
CHUNK = 256
B_HEADS = 8
B_HEAD_DIM = 128
D_FF = 5632
FFN_CONV = 3
EPS = 1e-6

A_QK = A_HEADS * A_QK_DIM
A_V = A_HEADS * A_V_DIM
B_QK = B_HEADS * 2 * B_HEAD_DIM
B_V = B_HEADS * 2 * B_HEAD_DIM
SPLIT_SIZES = (2 * A_QK, A_V, A_V, 2 * A_HEADS, B_QK, B_QK, B_V, D_MODEL, D_MODEL)

SUBLANES = 8
LANES = 128
LOG2E = math.log2(math.e)
HALO = SUBLANES
VMEM_LIMIT = 56 * 1024 * 1024


def _params(*sem):
    return pltpu.CompilerParams(dimension_semantics=sem, vmem_limit_bytes=VMEM_LIMIT)


def _dot(a, b):
    return jnp.dot(a, b, preferred_element_type=F32)


def _dot_nt(a, b):
    return lax.dot_general(a, b, (((1,), (1,)), ((), ())), preferred_element_type=F32)


def _sigmoid(x):
    return 1.0 / (1.0 + jnp.exp(-x))


def _lane_tile(x, width):
    return jnp.concatenate([x] * (width // LANES), axis=1)


def _rmsnorm_gates_kernel(x_ref, g_ref, w_ref, wt_ref, bc_ref, br_ref, o_ref, oc_ref, or_ref):
    x = x_ref[...]
    ms = jnp.mean(x * x, axis=-1, keepdims=True)
    h = (x * lax.rsqrt(ms + EPS) * g_ref[...]).astype(o_ref.dtype)
    o_ref[...] = h
    oc_ref[...] = _dot(h, w_ref[...]) + bc_ref[...]
    or_ref[...] = _dot_nt(wt_ref[...], h) + br_ref[...]


def _rmsnorm_gates(x, g, w_if, bias, tm):
    n, d = x.shape
    ng = w_if.shape[1]
    const = lambda r, c: pl.BlockSpec((r, c), lambda i: (0, 0))
    return pl.pallas_call(
        _rmsnorm_gates_kernel,
        grid=(n // tm,),
        in_specs=[pl.BlockSpec((tm, d), lambda i: (i, 0)), const(1, d),
                  const(d, ng), const(ng, d), const(1, ng), const(ng, 1)],
        out_specs=[pl.BlockSpec((tm, d), lambda i: (i, 0)),
                   pl.BlockSpec((tm, ng), lambda i: (i, 0)),
                   pl.BlockSpec((ng, tm), lambda i: (0, i))],
        out_shape=[jax.ShapeDtypeStruct((n, d), BF16),
                   jax.ShapeDtypeStruct((n, ng), F32), jax.ShapeDtypeStruct((ng, n), F32)],
        compiler_params=_params("parallel"),
        name="rmsnorm_gates",
    )(x, g.reshape(1, d), w_if, w_if.T, bias.reshape(1, ng), bias.reshape(ng, 1))


def _proj_plain_kernel(h_ref, w_ref, o_ref):
    o_ref[...] = _dot(h_ref[...], w_ref[...]).astype(o_ref.dtype)


def _proj_sigmoid_kernel(h_ref, w_ref, o_ref):
    o_ref[...] = _sigmoid(_dot(h_ref[...], w_ref[...])).astype(o_ref.dtype)


def _proj_headnorm_kernel(h_ref, w_ref, g_ref, o_ref, *, group):
    acc = _dot(h_ref[...], w_ref[...])
    for c in range(acc.shape[1] // group):
        sl = slice(c * group, (c + 1) * group)
        blk = acc[:, sl]
        ms = jnp.mean(blk * blk, axis=-1, keepdims=True)
        o_ref[:, sl] = (blk * lax.rsqrt(ms + EPS) * g_ref[:, sl]).astype(o_ref.dtype)


def _causal_conv(raw_ref, cw_ref, cb_ref, tm, kw):
    raw = raw_ref[...]
    y = cb_ref[...] + cw_ref[kw - 1:kw, :] * raw[HALO:, :]
    for d in range(1, kw):
        y = y + cw_ref[kw - 1 - d:kw - d, :] * pltpu.roll(raw, d, axis=0)[HALO:, :]
    return y


def _shift_halo(raw_ref, tm, tiles_per_seq):
    i = pl.program_id(1)

    @pl.when(i % tiles_per_seq == 0)
    def _():
        raw_ref[0:HALO, :] = jnp.zeros((HALO, raw_ref.shape[1]), F32)

    @pl.when(i % tiles_per_seq != 0)
    def _():
        raw_ref[0:HALO, :] = raw_ref[tm:tm + HALO, :]


def _proj_conv_silu_kernel(h_ref, w_ref, cw_ref, cb_ref, ps_ref, o_ref, raw_ref, *, tm, kw, tiles_per_seq):
    _shift_halo(raw_ref, tm, tiles_per_seq)
    raw_ref[HALO:HALO + tm, :] = _dot(h_ref[...], w_ref[...])
    y = _causal_conv(raw_ref, cw_ref, cb_ref, tm, kw)
    o_ref[...] = (y * _sigmoid(y) * ps_ref[...]).astype(o_ref.dtype)


def _proj(h, w, kind, tm, tn, extra=(), seq=None):
    n, k = h.shape
    m = w.shape[1]
    grid = (m // tn, n // tm)
    h_spec = pl.BlockSpec((tm, k), lambda j, i: (i, 0))
    w_spec = pl.BlockSpec((k, tn), lambda j, i: (0, j))
    o_spec = pl.BlockSpec((tm, tn), lambda j, i: (i, j))
    row_spec = lambda r: pl.BlockSpec((r, tn), lambda j, i: (0, j))
    scratch = []
    sem = ("parallel", "parallel")
    if kind == "plain":
        body, in_specs = _proj_plain_kernel, [h_spec, w_spec]
    elif kind == "sigmoid":
        body, in_specs = _proj_sigmoid_kernel, [h_spec, w_spec]
    elif kind == "headnorm":
        body = functools.partial(_proj_headnorm_kernel, group=B_HEAD_DIM)
        in_specs = [h_spec, w_spec, row_spec(1)]
    elif kind == "conv_silu":
        kw = extra[0].shape[0]
        body = functools.partial(_proj_conv_silu_kernel, tm=tm, kw=kw, tiles_per_seq=seq // tm)
        in_specs = [h_spec, w_spec, row_spec(kw), row_spec(1), row_spec(1)]
        scratch = [pltpu.VMEM((tm + HALO, tn), F32)]
        sem = ("parallel", "arbitrary")
    else:
        raise ValueError(kind)
    return pl.pallas_call(
        body,
        grid=grid,
        in_specs=in_specs,
        out_specs=o_spec,
        out_shape=jax.ShapeDtypeStruct((n, m), BF16),
        scratch_shapes=scratch,
        compiler_params=_params(*sem),
        name="proj_" + kind,
    )(h, w, *extra)


def _log_sigmoid(x):
    return -(jnp.maximum(-x, 0.0) + jnp.log1p(jnp.exp(-jnp.abs(x))))


def _mlstm_kernel(qk_ref, v_ref, gc_ref, gr_ref, so_ref, sg_ref, ng_ref, o_ref, *state):
    L = CHUNK
    H = A_HEADS
    c_refs, m_ref = state[:H], state[H]
    ones = jnp.ones((L, LANES), BF16)

    @pl.when(pl.program_id(1) == 0)
    def _():
        for r in state:
            r[...] = jnp.zeros(r.shape, F32)

    row = lax.broadcasted_iota(jnp.int32, (L, L), 0)
    col = lax.broadcasted_iota(jnp.int32, (L, L), 1)
    lower = (col <= row)
    tri_lower = lower.astype(F32)
    tri_upper = (row <= col).astype(F32)
    exact = functools.partial(jnp.dot, preferred_element_type=F32, precision=lax.Precision.HIGHEST)

    li_c = gc_ref[:, 0:H]
    b_c = exact(tri_lower, _log_sigmoid(gc_ref[:, H:2 * H]))
    m_st = m_ref[...]
    inter = b_c + m_st
    b_last = b_c[L - 1:L, :]
    ws = b_last - b_c + li_c
    m_new = jnp.maximum(b_last + m_st, jnp.max(ws, axis=0, keepdims=True))
    decay = jnp.exp(b_last + m_st - m_new)
    ws = jnp.exp(ws - m_new)
    m_ref[...] = m_new
    d_r = exact(_log_sigmoid(gr_ref[H:2 * H, :]), tri_upper) - gr_ref[0:H, :]

    hs = []
    for hd in range(H):
        q = qk_ref[:, hd * A_QK_DIM:(hd + 1) * A_QK_DIM]
        k = qk_ref[:, A_QK + hd * A_QK_DIM:A_QK + (hd + 1) * A_QK_DIM]
        v = jnp.concatenate([v_ref[:, hd * A_V_DIM:(hd + 1) * A_V_DIM], ones], axis=1)
        c_st = c_refs[hd][...]
        inter_h = inter[:, hd:hd + 1]

        dmat = jnp.where(lower, b_c[:, hd:hd + 1] - d_r[hd:hd + 1, :], -jnp.inf)
        m_t = jnp.maximum(jnp.max(dmat, axis=-1, keepdims=True), inter_h)
        w_in = jnp.exp(dmat - m_t)
        g_inter = jnp.exp(inter_h - m_t)
        p = _dot_nt(q, k) * w_in
        num = _dot(p.astype(BF16), v) + g_inter * _dot(q, c_st.astype(BF16))
        den = num[:, A_V_DIM:]
        inv = 1.0 / jnp.maximum(jnp.abs(den), jnp.exp(-m_t))
        hs.append(num[:, :A_V_DIM] * _lane_tile(inv, A_V_DIM))

        kw = k.astype(F32) * ws[:, hd:hd + 1]
        c_refs[hd][...] = decay[:, hd:hd + 1] * c_st + _dot(kw.T.astype(BF16), v)

    for hd in range(H):
        sl = slice(hd * A_V_DIM, (hd + 1) * A_V_DIM)
        hh = hs[hd]
        ms = jnp.mean(hh * hh, axis=-1, keepdims=True)
        y = hh * lax.rsqrt(ms + EPS) * ng_ref[:, sl]
        gate = so_ref[:, sl].astype(F32) * sg_ref[:, sl].astype(F32)
        o_ref[:, sl] = (gate * y).astype(o_ref.dtype)


def _mlstm(qk, vv, g_col, g_row, sig, norm_g, bsz, seq):
    n = qk.shape[0]
    nc = seq // CHUNK
    rows = lambda b, c: b * nc + c
    return pl.pallas_call(
        _mlstm_kernel,
        grid=(bsz, nc),
        in_specs=[pl.BlockSpec((CHUNK, 2 * A_QK), lambda b, c: (rows(b, c), 0)),
                  pl.BlockSpec((CHUNK, A_V), lambda b, c: (rows(b, c), 0)),
                  pl.BlockSpec((CHUNK, 2 * A_HEADS), lambda b, c: (rows(b, c), 0)),
                  pl.BlockSpec((2 * A_HEADS, CHUNK), lambda b, c: (0, rows(b, c))),
                  pl.BlockSpec((CHUNK, A_V), lambda b, c: (rows(b, c), 0)),
                  pl.BlockSpec((CHUNK, A_V), lambda b, c: (rows(b, c), 1)),
                  pl.BlockSpec((1, A_V), lambda b, c: (0, 0))],
        out_specs=pl.BlockSpec((CHUNK, A_V), lambda b, c: (rows(b, c), 0)),
        out_shape=jax.ShapeDtypeStruct((n, A_V), BF16),
        scratch_shapes=([pltpu.VMEM((A_QK_DIM, A_V_DIM + LANES), F32)] * A_HEADS
                        + [pltpu.VMEM((1, A_HEADS), F32)]),
        compiler_params=_params("parallel", "arbitrary"),
        name="mlstm",
    )(qk, vv, g_col, g_row, sig, sig, norm_g.reshape(1, A_V))


def _diff_attn_kernel(slope_ref, q_ref, k_ref, v_ref, sg_ref, lp_ref, ng_ref, o_ref,
                      m_ref, l_ref, acc_ref, *, tq, tk, lam_init):
    dh = B_HEAD_DIM
    hw = 2 * dh
    hd = pl.program_id(1)
    i = pl.program_id(2)
    slope = slope_ref[hd]
    m_ref[...] = jnp.full(m_ref.shape, -jnp.inf, F32)
    l_ref[...] = jnp.zeros(l_ref.shape, F32)
    acc_ref[...] = jnp.zeros(acc_ref.shape, F32)
    key_pos = lax.broadcasted_iota(jnp.int32, (1, tk), 1).astype(F32)

    def step(j, masked):
        start = pl.multiple_of(j * tk, tk)
        kblk = k_ref[pl.ds(start, tk), :]
        vblk = v_ref[pl.ds(start, tk), :]
        bias = slope * (key_pos - ((i - j) * tk).astype(F32))
        ps, alphas = [], []
        for c in range(2):
            s = _dot_nt(q_ref[:, c * dh:(c + 1) * dh], kblk[:, c * dh:(c + 1) * dh]) + bias
            if masked:
                r = lax.broadcasted_iota(jnp.int32, (tq, tk), 0)
                cc = lax.broadcasted_iota(jnp.int32, (tq, tk), 1)
                s = jnp.where(cc <= r, s, -jnp.inf)
            m_old = m_ref[c]
            m_new = jnp.maximum(m_old, jnp.max(s, axis=-1, keepdims=True))
            alpha = jnp.exp2(m_old - m_new)
            p = jnp.exp2(s - _lane_tile(m_new, tk))
            l_ref[c] = alpha * l_ref[c] + jnp.sum(p, axis=-1, keepdims=True)
            m_ref[c] = m_new
            ps.append(p.astype(BF16))
            alphas.append(alpha)
        pv = _dot(jnp.concatenate(ps, axis=0), vblk)
        for c in range(2):
            acc_ref[c] = _lane_tile(alphas[c], hw) * acc_ref[c] + pv[c * tq:(c + 1) * tq]

    def body(j, carry):
        step(j, False)
        return carry

    lax.fori_loop(0, i, body, 0)
    step(i, True)

    _diff_attn_finish(acc_ref, 1.0 / l_ref[0], 1.0 / l_ref[1], sg_ref, lp_ref, ng_ref, o_ref, lam_init)


def _diff_attn_finish(acc_ref, inv0, inv1, sg_ref, lp_ref, ng_ref, o_ref, lam_init):
    hw = acc_ref.shape[-1]
    lp = lp_ref[...]
    lam = (jnp.exp(jnp.sum(lp[0:1] * lp[1:2], axis=-1, keepdims=True))
           - jnp.exp(jnp.sum(lp[2:3] * lp[3:4], axis=-1, keepdims=True)) + lam_init)
    o = acc_ref[0] * _lane_tile(inv0, hw) - acc_ref[1] * _lane_tile(lam * inv1, hw)
    ms = jnp.mean(o * o, axis=-1, keepdims=True)
    y = o * lax.rsqrt(ms + EPS) * (ng_ref[...] * (1.0 - lam_init))
    o_ref[...] = (sg_ref[...].astype(F32) * y).astype(o_ref.dtype)


FIXED_OFFSET_MAX = 48.0
SPLIT_PIECES = 3
KEY_POS_RADIX = 64


def _split_bf16(x):
    pieces = []
    for _ in range(SPLIT_PIECES):
        p = x.astype(BF16).astype(F32)
        pieces.append(p)
        x = x - p
    return pieces


def _key_features(seq):
    s = np.arange(seq)
    f = np.zeros((seq, LANES), np.float32)
    for t in range(SPLIT_PIECES):
        f[:, 2 * t] = (s // KEY_POS_RADIX) * KEY_POS_RADIX
        f[:, 2 * t + 1] = s % KEY_POS_RADIX
    f[:, 2 * SPLIT_PIECES:3 * SPLIT_PIECES] = 1.0
    return jnp.asarray(f, dtype=BF16)


def _diff_attn_fixed_kernel(tab_ref, q_ref, k_ref, kf_ref, v_ref, sg_ref, lp_ref, ng_ref, o_ref,
                            l_ref, acc_ref, qa_ref, *, tq, tk, lam_init):
    dh = B_HEAD_DIM
    hd = pl.program_id(1)
    i = pl.program_id(2)
    slope = tab_ref[hd]
    offset = tab_ref[B_HEADS]
    l_ref[...] = jnp.zeros(l_ref.shape, F32)
    acc_ref[...] = jnp.zeros(acc_ref.shape, F32)

    lane = lax.broadcasted_iota(jnp.int32, (tq, LANES), 1)
    t_pos = (lax.broadcasted_iota(jnp.int32, (tq, LANES), 0) + i * tq).astype(F32)
    slope_parts = _split_bf16(jnp.full((tq, LANES), slope, F32))
    row_parts = _split_bf16(-(slope * t_pos) - offset)
    feat = jnp.zeros((tq, LANES), F32)
    for t in range(SPLIT_PIECES):
        feat = jnp.where(jnp.right_shift(lane, 1) == t, slope_parts[t], feat)
        feat = jnp.where(lane == 2 * SPLIT_PIECES + t, row_parts[t], feat)
    feat = feat.astype(BF16)
    for c in range(2):
        qa_ref[c] = jnp.concatenate([q_ref[:, c * dh:(c + 1) * dh], feat], axis=1)

    def tile(start, row0, nrows, nkeys, mask_shift):
        rows = slice(row0, row0 + nrows)
        kblk = k_ref[pl.ds(start, nkeys), :]
        kfblk = kf_ref[pl.ds(start, nkeys), :]
        vblk = v_ref[pl.ds(start, nkeys), :]
        ps = []
        for c in range(2):
            ka = jnp.concatenate([kblk[:, c * dh:(c + 1) * dh], kfblk], axis=1)
            s = _dot_nt(qa_ref[c, rows, :], ka)
            if mask_shift is not None:
                r = lax.broadcasted_iota(jnp.int32, (nrows, nkeys), 0) + mask_shift
                cc = lax.broadcasted_iota(jnp.int32, (nrows, nkeys), 1)
                s = jnp.where(cc <= r, s, -jnp.inf)
            p = jnp.exp2(s)
            part = p[:, 0:LANES]
            for g in range(1, nkeys // LANES):
                part = part + p[:, g * LANES:(g + 1) * LANES]
            l_ref[c, rows, :] += part
            ps.append(p.astype(BF16))
        pv = _dot(jnp.concatenate(ps, axis=0), vblk)
        for c in range(2):
            acc_ref[c, rows, :] += pv[c * nrows:(c + 1) * nrows]

    def body(j, carry):
        tile(pl.multiple_of(j * tk, tk), 0, tq, tk, None)
        return carry

    lax.fori_loop(0, i, body, 0)
    diag = pl.multiple_of(i * tk, tk)
    half = tq // 2
    tile(diag, 0, half, half, 0)
    tile(diag, half, half, tk, half)

    inv = [jnp.broadcast_to(1.0 / jnp.sum(l_ref[c], axis=-1, keepdims=True), (tq, LANES)) for c in range(2)]
    _diff_attn_finish(acc_ref, inv[0], inv[1], sg_ref, lp_ref, ng_ref, o_ref, lam_init)


def _diff_attn(qk, vv, sig, lam_params, subln_g, lam_init, bsz, seq, tq, offset=None):
    n = qk.shape[0]
    hw = 2 * B_HEAD_DIM
    nq = seq // tq
    slopes = jnp.asarray(2.0 ** (-8.0 * np.arange(1, B_HEADS + 1) / B_HEADS) * LOG2E, dtype=F32)
    q_cols = B_QK // hw
    v_cols = A_V // hw
    g_cols = 2 * A_V // hw
    q_spec = pl.BlockSpec((tq, hw), lambda b, h, i, s: (b * nq + i, h))
    k_spec = pl.BlockSpec((seq, hw), lambda b, h, i, s: (b, q_cols + h))
    tail_specs = [pl.BlockSpec((seq, hw), lambda b, h, i, s: (b, v_cols + h)),
                  pl.BlockSpec((tq, hw), lambda b, h, i, s: (b * nq + i, g_cols + h)),
                  pl.BlockSpec((4, B_HEAD_DIM), lambda b, h, i, s: (0, 0)),
                  pl.BlockSpec((1, hw), lambda b, h, i, s: (0, 0))]
    tail_args = (vv, sig, lam_params, subln_g.reshape(1, hw))
    stats = pltpu.VMEM((2, tq, LANES), F32)
    acc = pltpu.VMEM((2, tq, hw), F32)
    if offset is None:
        kernel = functools.partial(_diff_attn_kernel, tq=tq, tk=tq, lam_init=lam_init)
        table, in_specs, args = slopes, [q_spec, k_spec] + tail_specs, (qk, qk) + tail_args
        scratch = [stats, stats, acc]
    else:
        kernel = functools.partial(_diff_attn_fixed_kernel, tq=tq, tk=tq, lam_init=lam_init)
        table = jnp.concatenate([slopes, jnp.reshape(offset, (1,)).astype(F32)])
        kf_spec = pl.BlockSpec((seq, LANES), lambda b, h, i, s: (0, 0))
        in_specs, args = [q_spec, k_spec, kf_spec] + tail_specs, (qk, qk, _key_features(seq)) + tail_args
        scratch = [stats, acc, pltpu.VMEM((2, tq, hw), BF16)]
    grid_spec = pltpu.PrefetchScalarGridSpec(
        num_scalar_prefetch=1,
        grid=(bsz, B_HEADS, nq),
        in_specs=in_specs,
        out_specs=pl.BlockSpec((tq, hw), lambda b, h, i, s: (b * nq + i, h)),
        scratch_shapes=scratch,
    )
    return pl.pallas_call(
        kernel,
        grid_spec=grid_spec,
        out_shape=jax.ShapeDtypeStruct((n, B_V), BF16),
        compiler_params=_params("parallel", "parallel", "parallel"),
        name="diff_attn" if offset is None else "diff_attn_fixed",
    )(table, *args)


def _out_proj_kernel(ya_ref, yb_ref, w_ref, x_ref, g_ref, x1_ref, h2_ref):
    y = (ya_ref[...].astype(F32) + yb_ref[...].astype(F32)).astype(BF16)
    x1 = x_ref[...] + _dot(y, w_ref[...])
    x1_ref[...] = x1
    ms = jnp.mean(x1 * x1, axis=-1, keepdims=True)
    h2_ref[...] = (x1 * lax.rsqrt(ms + EPS) * g_ref[...]).astype(h2_ref.dtype)


def _out_proj(ya, yb, w, x, g, tm):
    n, d = x.shape
    row = pl.BlockSpec((tm, d), lambda i: (i, 0))
    return pl.pallas_call(
        _out_proj_kernel,
        grid=(n // tm,),
        in_specs=[row, row, pl.BlockSpec((d, d), lambda i: (0, 0)), row,
                  pl.BlockSpec((1, d), lambda i: (0, 0))],
        out_specs=[row, row],
        out_shape=[jax.ShapeDtypeStruct((n, d), F32), jax.ShapeDtypeStruct((n, d), BF16)],
        compiler_params=_params("parallel"),
        name="out_proj",
    )(ya, yb, w, x, g.reshape(1, d))


def _ffn_up_kernel(h_ref, wg_ref, wv_ref, cwg_ref, cwv_ref, cbg_ref, cbv_ref, o_ref,
                   rg_ref, rv_ref, *, tm, kw, tiles_per_seq):
    _shift_halo(rg_ref, tm, tiles_per_seq)
    _shift_halo(rv_ref, tm, tiles_per_seq)
    h = h_ref[...]
    rg_ref[HALO:HALO + tm, :] = _dot(h, wg_ref[...])
    rv_ref[HALO:HALO + tm, :] = _dot(h, wv_ref[...])
    ug = _causal_conv(rg_ref, cwg_ref, cbg_ref, tm, kw)
    uv = _causal_conv(rv_ref, cwv_ref, cbv_ref, tm, kw)
    o_ref[...] = (ug * _sigmoid(ug) * uv).astype(o_ref.dtype)


def _ffn_up(h, w_up, conv_w, conv_b, seq, tm, tn):
    n, k = h.shape
    nj = D_FF // tn
    kw = conv_w.shape[0]
    gate = lambda r: pl.BlockSpec((r, tn), lambda j, i: (0, j))
    val = lambda r: pl.BlockSpec((r, tn), lambda j, i: (0, nj + j))
    kernel = functools.partial(_ffn_up_kernel, tm=tm, kw=kw, tiles_per_seq=seq // tm)
    return pl.pallas_call(
        kernel,
        grid=(nj, n // tm),
        in_specs=[pl.BlockSpec((tm, k), lambda j, i: (i, 0)),
                  gate(k), val(k), gate(kw), val(kw), gate(1), val(1)],
        out_specs=pl.BlockSpec((tm, tn), lambda j, i: (i, j)),
        out_shape=jax.ShapeDtypeStruct((n, D_FF), BF16),
        scratch_shapes=[pltpu.VMEM((tm + HALO, tn), F32), pltpu.VMEM((tm + HALO, tn), F32)],
        compiler_params=_params("parallel", "arbitrary"),
        name="ffn_up",
    )(h, w_up, w_up, conv_w, conv_w, conv_b, conv_b)


def _ffn_down_kernel(a_ref, w_ref, x_ref, o_ref):
    k = pl.program_id(1)

    @pl.when(k == 0)
    def _():
        o_ref[...] = x_ref[...]

    o_ref[...] += _dot(a_ref[...], w_ref[...])


def _ffn_down(act, w, x1, tm, tk):
    n, d = x1.shape
    return pl.pallas_call(
        _ffn_down_kernel,
        grid=(n // tm, D_FF // tk),
        in_specs=[pl.BlockSpec((tm, tk), lambda i, k: (i, k)),
                  pl.BlockSpec((tk, d), lambda i, k: (k, 0)),
                  pl.BlockSpec((tm, d), lambda i, k: (i, 0))],
        out_specs=pl.BlockSpec((tm, d), lambda i, k: (i, 0)),
        out_shape=jax.ShapeDtypeStruct((n, d), F32),
        compiler_params=_params("parallel", "arbitrary"),
        name="ffn_down",
    )(act, w, x1)


def _layer(x2, bsz, seq, l, norm1_g, w_in, if_bias, qk_conv_w, qk_conv_b, mlstm_norm_g, q_norm_g,
           k_norm_g, diff_lambda, subln_g, w_out, norm2_g, w_up, ffn_conv_w, ffn_conv_b, w_down):
    n = bsz * seq
    tm = min(1024, seq)
    off = np.cumsum((0,) + SPLIT_SIZES)
    cols = lambda g: w_in[:, off[g]:off[g + 1]]
    w_aqk = cols(0).astype(BF16)
    w_plain = jnp.concatenate([cols(1), cols(6)], axis=1).astype(BF16)
    w_sig = jnp.concatenate([cols(2), cols(7), cols(8)], axis=1).astype(BF16)
    w_if = cols(3).astype(BF16)
    w_bqk = jnp.concatenate([cols(4), cols(5)], axis=1).astype(BF16)

    h, g_col, g_row = _rmsnorm_gates(x2, norm1_g, w_if, if_bias, min(512, seq))

    q_scale = jnp.concatenate([jnp.full((A_QK,), A_QK_DIM ** -0.5, F32), jnp.ones((A_QK,), F32)])
    qk_m = _proj(h, w_aqk, "conv_silu", tm, 512,
                 extra=(qk_conv_w, qk_conv_b.reshape(1, -1), q_scale.reshape(1, -1)), seq=seq)
    vv = _proj(h, w_plain, "plain", tm, 2048)
    sig = _proj(h, w_sig, "sigmoid", tm, 2048)
    reps = B_QK // B_HEAD_DIM
    qk_gain = jnp.concatenate([jnp.tile(q_norm_g * (B_HEAD_DIM ** -0.5 * LOG2E), reps), jnp.tile(k_norm_g, reps)])
    qk_d = _proj(h, w_bqk, "headnorm", tm, 2048, extra=(qk_gain.reshape(1, -1),))

    y_a = _mlstm(qk_m, vv, g_col, g_row, sig, mlstm_norm_g, bsz, seq)
    lam_init = 0.8 - 0.6 * math.exp(-0.3 * l)
    logit_bound = 1.01 * B_HEAD_DIM ** 0.5 * LOG2E * jnp.max(jnp.abs(q_norm_g * k_norm_g))
    attn = functools.partial(_diff_attn, qk_d, vv, sig, diff_lambda, subln_g, lam_init, bsz, seq)
    y_b = lax.cond(logit_bound <= FIXED_OFFSET_MAX,
                   lambda: attn(min(1024, seq), logit_bound),
                   lambda: attn(min(512, seq)))

    x1, h2 = _out_proj(y_a, y_b, w_out.astype(BF16), x2, norm2_g, min(512, seq))
    act = _ffn_up(h2, w_up.astype(BF16), ffn_conv_w, ffn_conv_b.reshape(1, -1), seq, tm, 512)
    return _ffn_down(act, w_down.astype(BF16), x1, tm, D_FF // 4)


def kernel(x, norm1_g, w_in, if_bias, qk_conv_w, qk_conv_b, mlstm_norm_g, q_norm_g, k_norm_g,
           diff_lambda, subln_g, w_out, norm2_g, w_up, ffn_conv_w, ffn_conv_b, w_down):
    bsz, seq, d = x.shape
    x2 = x.reshape(bsz * seq, d)
    for l in range(norm1_g.shape[0]):
        x2 = _layer(x2, bsz, seq, l, norm1_g[l], w_in[l], if_bias[l], qk_conv_w[l], qk_conv_b[l],
                    mlstm_norm_g[l], q_norm_g[l], k_norm_g[l], diff_lambda[l], subln_g[l], w_out[l],
                    norm2_g[l], w_up[l], ffn_conv_w[l], ffn_conv_b[l], w_down[l])
    return x2.reshape(bsz, seq, d)
```

```python
import functools
import math

import numpy as np
import jax
import jax.numpy as jnp
from jax import lax
from jax.experimental import pallas as pl
from jax.experimental.pallas import tpu as pltpu

F32 = jnp.float32
BF16 = jnp.bfloat16

D_MODEL = 2048
A_HEADS = 4
A_QK_DIM = 256
A_V_DIM = 512
A_CONV = 4
CHUNK = 256
B_HEADS = 8
B_HEAD_DIM = 128
D_FF = 5632
FFN_CONV = 3
EPS = 1e-6

A_QK = A_HEADS * A_QK_DIM
A_V = A_HEADS * A_V_DIM
B_QK = B_HEADS * 2 * B_HEAD_DIM
B_V = B_HEADS * 2 * B_HEAD_DIM
SPLIT_SIZES = (2 * A_QK, A_V, A_V, 2 * A_HEADS, B_QK, B_QK, B_V, D_MODEL, D_MODEL)

SUBLANES = 8
LANES = 128
LOG2E = math.log2(math.e)
HALO = SUBLANES
VMEM_LIMIT = 56 * 1024 * 1024


def _params(*sem):
    return pltpu.CompilerParams(dimension_semantics=sem, vmem_limit_bytes=VMEM_LIMIT)


def _dot(a, b):
    return jnp.dot(a, b, preferred_element_type=F32)


def _dot_nt(a, b):
    return lax.dot_general(a, b, (((1,), (1,)), ((), ())), preferred_element_type=F32)


def _sigmoid(x):
    return 1.0 / (1.0 + jnp.exp(-x))


def _lane_tile(x, width):
    return jnp.concatenate([x] * (width // LANES), axis=1)


def _rmsnorm_gates_kernel(x_ref, g_ref, w_ref, wt_ref, bc_ref, br_ref, o_ref, oc_ref, or_ref):
    x = x_ref[...]
    ms = jnp.mean(x * x, axis=-1, keepdims=True)
    h = (x * lax.rsqrt(ms + EPS) * g_ref[...]).astype(o_ref.dtype)
    o_ref[...] = h
    oc_ref[...] = _dot(h, w_ref[...]) + bc_ref[...]
    or_ref[...] = _dot_nt(wt_ref[...], h) + br_ref[...]


def _rmsnorm_gates(x, g, w_if, bias, tm):
    n, d = x.shape
    ng = w_if.shape[1]
    const = lambda r, c: pl.BlockSpec((r, c), lambda i: (0, 0))
    return pl.pallas_call(
        _rmsnorm_gates_kernel,
        grid=(n // tm,),
        in_specs=[pl.BlockSpec((tm, d), lambda i: (i, 0)), const(1, d),
                  const(d, ng), const(ng, d), const(1, ng), const(ng, 1)],
        out_specs=[pl.BlockSpec((tm, d), lambda i: (i, 0)),
                   pl.BlockSpec((tm, ng), lambda i: (i, 0)),
                   pl.BlockSpec((ng, tm), lambda i: (0, i))],
        out_shape=[jax.ShapeDtypeStruct((n, d), BF16),
                   jax.ShapeDtypeStruct((n, ng), F32), jax.ShapeDtypeStruct((ng, n), F32)],
        compiler_params=_params("parallel"),
        name="rmsnorm_gates",
    )(x, g.reshape(1, d), w_if, w_if.T, bias.reshape(1, ng), bias.reshape(ng, 1))


def _proj_plain_kernel(h_ref, w_ref, o_ref):
    o_ref[...] = _dot(h_ref[...], w_ref[...]).astype(o_ref.dtype)


def _proj_sigmoid_kernel(h_ref, w_ref, o_ref):
    o_ref[...] = _sigmoid(_dot(h_ref[...], w_ref[...])).astype(o_ref.dtype)


def _proj_headnorm_kernel(h_ref, w_ref, g_ref, o_ref, *, group):
    acc = _dot(h_ref[...], w_ref[...])
    for c in range(acc.shape[1] // group):
        sl = slice(c * group, (c + 1) * group)
        blk = acc[:, sl]
        ms = jnp.mean(blk * blk, axis=-1, keepdims=True)
        o_ref[:, sl] = (blk * lax.rsqrt(ms + EPS) * g_ref[:, sl]).astype(o_ref.dtype)


def _causal_conv(raw_ref, cw_ref, cb_ref, tm, kw):
    raw = raw_ref[...]
    y = cb_ref[...] + cw_ref[kw - 1:kw, :] * raw[HALO:, :]
    for d in range(1, kw):
        y = y + cw_ref[kw - 1 - d:kw - d, :] * pltpu.roll(raw, d, axis=0)[HALO:, :]
    return y


def _shift_halo(raw_ref, tm, tiles_per_seq):
    i = pl.program_id(1)

    @pl.when(i % tiles_per_seq == 0)
    def _():
        raw_ref[0:HALO, :] = jnp.zeros((HALO, raw_ref.shape[1]), F32)

    @pl.when(i % tiles_per_seq != 0)
    def _():
        raw_ref[0:HALO, :] = raw_ref[tm:tm + HALO, :]


def _proj_conv_silu_kernel(h_ref, w_ref, cw_ref, cb_ref, ps_ref, o_ref, raw_ref, *, tm, kw, tiles_per_seq):
    _shift_halo(raw_ref, tm, tiles_per_seq)
    raw_ref[HALO:HALO + tm, :] = _dot(h_ref[...], w_ref[...])
    y = _causal_conv(raw_ref, cw_ref, cb_ref, tm, kw)
    o_ref[...] = (y * _sigmoid(y) * ps_ref[...]).astype(o_ref.dtype)


def _proj(h, w, kind, tm, tn, extra=(), seq=None):
    n, k = h.shape
    m = w.shape[1]
    grid = (m // tn, n // tm)
    h_spec = pl.BlockSpec((tm, k), lambda j, i: (i, 0))
    w_spec = pl.BlockSpec((k, tn), lambda j, i: (0, j))
    o_spec = pl.BlockSpec((tm, tn), lambda j, i: (i, j))
    row_spec = lambda r: pl.BlockSpec((r, tn), lambda j, i: (0, j))
    scratch = []
    sem = ("parallel", "parallel")
    if kind == "plain":
        body, in_specs = _proj_plain_kernel, [h_spec, w_spec]
    elif kind == "sigmoid":
        body, in_specs = _proj_sigmoid_kernel, [h_spec, w_spec]
    elif kind == "headnorm":
        body = functools.partial(_proj_headnorm_kernel, group=B_HEAD_DIM)
        in_specs = [h_spec, w_spec, row_spec(1)]
    elif kind == "conv_silu":
        kw = extra[0].shape[0]
        body = functools.partial(_proj_conv_silu_kernel, tm=tm, kw=kw, tiles_per_seq=seq // tm)
        in_specs = [h_spec, w_spec, row_spec(kw), row_spec(1), row_spec(1)]
        scratch = [pltpu.VMEM((tm + HALO, tn), F32)]
        sem = ("parallel", "arbitrary")
    else:
        raise ValueError(kind)
    return pl.pallas_call(
        body,
        grid=grid,
        in_specs=in_specs,
        out_specs=o_spec,
        out_shape=jax.ShapeDtypeStruct((n, m), BF16),
        scratch_shapes=scratch,
        compiler_params=_params(*sem),
        name="proj_" + kind,
    )(h, w, *extra)


def _log_sigmoid(x):
    return -(jnp.maximum(-x, 0.0) + jnp.log1p(jnp.exp(-jnp.abs(x))))


def _mlstm_kernel(qk_ref, v_ref, gc_ref, gr_ref, so_ref, sg_ref, ng_ref, o_ref, *state):
    L = CHUNK
    H = A_HEADS
    c_refs, m_ref = state[:H], state[H]
    ones = jnp.ones((L, LANES), BF16)

    @pl.when(pl.program_id(1) == 0)
    def _():
        for r in state:
            r[...] = jnp.zeros(r.shape, F32)

    row = lax.broadcasted_iota(jnp.int32, (L, L), 0)
    col = lax.broadcasted_iota(jnp.int32, (L, L), 1)
    lower = (col <= row)
    tri_lower = lower.astype(F32)
    tri_upper = (row <= col).astype(F32)
    exact = functools.partial(jnp.dot, preferred_element_type=F32, precision=lax.Precision.HIGHEST)

    li_c = gc_ref[:, 0:H]
    b_c = exact(tri_lower, _log_sigmoid(gc_ref[:, H:2 * H]))
    m_st = m_ref[...]
    inter = b_c + m_st
    b_last = b_c[L - 1:L, :]
    ws = b_last - b_c + li_c
    m_new = jnp.maximum(b_last + m_st, jnp.max(ws, axis=0, keepdims=True))
    decay = jnp.exp(b_last + m_st - m_new)
    ws = jnp.exp(ws - m_new)
    m_ref[...] = m_new
    d_r = exact(_log_sigmoid(gr_ref[H:2 * H, :]), tri_upper) - gr_ref[0:H, :]

    hs = []
    for hd in range(H):
        q = qk_ref[:, hd * A_QK_DIM:(hd + 1) * A_QK_DIM]
        k = qk_ref[:, A_QK + hd * A_QK_DIM:A_QK + (hd + 1) * A_QK_DIM]
        v = jnp.concatenate([v_ref[:, hd * A_V_DIM:(hd + 1) * A_V_DIM], ones], axis=1)
        c_st = c_refs[hd][...]
        inter_h = inter[:, hd:hd + 1]

        dmat = jnp.where(lower, b_c[:, hd:hd + 1] - d_r[hd:hd + 1, :], -jnp.inf)
        m_t = jnp.maximum(jnp.max(dmat, axis=-1, keepdims=True), inter_h)
        w_in = jnp.exp(dmat - m_t)
        g_inter = jnp.exp(inter_h - m_t)
        p = _dot_nt(q, k) * w_in
        num = _dot(p.astype(BF16), v) + g_inter * _dot(q, c_st.astype(BF16))
        den = num[:, A_V_DIM:]
        inv = 1.0 / jnp.maximum(jnp.abs(den), jnp.exp(-m_t))
        hs.append(num[:, :A_V_DIM] * _lane_tile(inv, A_V_DIM))

        kw = k.astype(F32) * ws[:, hd:hd + 1]
        c_refs[hd][...] = decay[:, hd:hd + 1] * c_st + _dot(kw.T.astype(BF16), v)

    for hd in range(H):
        sl = slice(hd * A_V_DIM, (hd + 1) * A_V_DIM)
        hh = hs[hd]
        ms = jnp.mean(hh * hh, axis=-1, keepdims=True)
        y = hh * lax.rsqrt(ms + EPS) * ng_ref[:, sl]
        gate = so_ref[:, sl].astype(F32) * sg_ref[:, sl].astype(F32)
        o_ref[:, sl] = (gate * y).astype(o_ref.dtype)


def _mlstm(qk, vv, g_col, g_row, sig, norm_g, bsz, seq):
    n = qk.shape[0]
    nc = seq // CHUNK
    rows = lambda b, c: b * nc + c
    return pl.pallas_call(
        _mlstm_kernel,
        grid=(bsz, nc),
        in_specs=[pl.BlockSpec((CHUNK, 2 * A_QK), lambda b, c: (rows(b, c), 0)),
                  pl.BlockSpec((CHUNK, A_V), lambda b, c: (rows(b, c), 0)),
                  pl.BlockSpec((CHUNK, 2 * A_HEADS), lambda b, c: (rows(b, c), 0)),
                  pl.BlockSpec((2 * A_HEADS, CHUNK), lambda b, c: (0, rows(b, c))),
                  pl.BlockSpec((CHUNK, A_V), lambda b, c: (rows(b, c), 0)),
                  pl.BlockSpec((CHUNK, A_V), lambda b, c: (rows(b, c), 1)),
                  pl.BlockSpec((1, A_V), lambda b, c: (0, 0))],
        out_specs=pl.BlockSpec((CHUNK, A_V), lambda b, c: (rows(b, c), 0)),
        out_shape=jax.ShapeDtypeStruct((n, A_V), BF16),
        scratch_shapes=([pltpu.VMEM((A_QK_DIM, A_V_DIM + LANES), F32)] * A_HEADS
                        + [pltpu.VMEM((1, A_HEADS), F32)]),
        compiler_params=_params("parallel", "arbitrary"),
        name="mlstm",
    )(qk, vv, g_col, g_row, sig, sig, norm_g.reshape(1, A_V))


def _diff_attn_kernel(slope_ref, q_ref, k_ref, v_ref, sg_ref, lp_ref, ng_ref, o_ref,
                      m_ref, l_ref, acc_ref, *, tq, tk, lam_init):
    dh = B_HEAD_DIM
    hw = 2 * dh
    hd = pl.program_id(1)
    i = pl.program_id(2)
    slope = slope_ref[hd]
    m_ref[...] = jnp.full(m_ref.shape, -jnp.inf, F32)
    l_ref[...] = jnp.zeros(l_ref.shape, F32)
    acc_ref[...] = jnp.zeros(acc_ref.shape, F32)
    key_pos = lax.broadcasted_iota(jnp.int32, (1, tk), 1).astype(F32)

    def step(j, masked):
        start = pl.multiple_of(j * tk, tk)
        kblk = k_ref[pl.ds(start, tk), :]
        vblk = v_ref[pl.ds(start, tk), :]
        bias = slope * (key_pos - ((i - j) * tk).astype(F32))
        ps, alphas = [], []
        for c in range(2):
            s = _dot_nt(q_ref[:, c * dh:(c + 1) * dh], kblk[:, c * dh:(c + 1) * dh]) + bias
            if masked:
                r = lax.broadcasted_iota(jnp.int32, (tq, tk), 0)
                cc = lax.broadcasted_iota(jnp.int32, (tq, tk), 1)
                s = jnp.where(cc <= r, s, -jnp.inf)
            m_old = m_ref[c]
            m_new = jnp.maximum(m_old, jnp.max(s, axis=-1, keepdims=True))
            alpha = jnp.exp2(m_old - m_new)
            p = jnp.exp2(s - _lane_tile(m_new, tk))
            l_ref[c] = alpha * l_ref[c] + jnp.sum(p, axis=-1, keepdims=True)
            m_ref[c] = m_new
            ps.append(p.astype(BF16))
            alphas.append(alpha)
        pv = _dot(jnp.concatenate(ps, axis=0), vblk)
        for c in range(2):
            acc_ref[c] = _lane_tile(alphas[c], hw) * acc_ref[c] + pv[c * tq:(c + 1) * tq]

    def body(j, carry):
        step(j, False)
        return carry

    lax.fori_loop(0, i, body, 0)
    step(i, True)

    _diff_attn_finish(acc_ref, 1.0 / l_ref[0], 1.0 / l_ref[1], sg_ref, lp_ref, ng_ref, o_ref, lam_init)


def _diff_attn_finish(acc_ref, inv0, inv1, sg_ref, lp_ref, ng_ref, o_ref, lam_init):
    hw = acc_ref.shape[-1]
    lp = lp_ref[...]
    lam = (jnp.exp(jnp.sum(lp[0:1] * lp[1:2], axis=-1, keepdims=True))
           - jnp.exp(jnp.sum(lp[2:3] * lp[3:4], axis=-1, keepdims=True)) + lam_init)
    o = acc_ref[0] * _lane_tile(inv0, hw) - acc_ref[1] * _lane_tile(lam * inv1, hw)
    ms = jnp.mean(o * o, axis=-1, keepdims=True)
    y = o * lax.rsqrt(ms + EPS) * (ng_ref[...] * (1.0 - lam_init))
    o_ref[...] = (sg_ref[...].astype(F32) * y).astype(o_ref.dtype)


FIXED_OFFSET_MAX = 48.0
SPLIT_PIECES = 3
KEY_POS_RADIX = 64


def _split_bf16(x):
    pieces = []
    for _ in range(SPLIT_PIECES):
        p = x.astype(BF16).astype(F32)
        pieces.append(p)
        x = x - p
    return pieces


def _key_features(seq):
    s = np.arange(seq)
    f = np.zeros((seq, LANES), np.float32)
    for t in range(SPLIT_PIECES):
        f[:, 2 * t] = (s // KEY_POS_RADIX) * KEY_POS_RADIX
        f[:, 2 * t + 1] = s % KEY_POS_RADIX
    f[:, 2 * SPLIT_PIECES:3 * SPLIT_PIECES] = 1.0
    return jnp.asarray(f, dtype=BF16)


def _diff_attn_fixed_kernel(tab_ref, q_ref, k_ref, kf_ref, v_ref, sg_ref, lp_ref, ng_ref, o_ref,
                            l_ref, acc_ref, qa_ref, *, tq, tk, lam_init):
    dh = B_HEAD_DIM
    hd = pl.program_id(1)
    i = pl.program_id(2)
    slope = tab_ref[hd]
    offset = tab_ref[B_HEADS]
    l_ref[...] = jnp.zeros(l_ref.shape, F32)
    acc_ref[...] = jnp.zeros(acc_ref.shape, F32)

    lane = lax.broadcasted_iota(jnp.int32, (tq, LANES), 1)
    t_pos = (lax.broadcasted_iota(jnp.int32, (tq, LANES), 0) + i * tq).astype(F32)
    slope_parts = _split_bf16(jnp.full((tq, LANES), slope, F32))
    row_parts = _split_bf16(-(slope * t_pos) - offset)
    feat = jnp.zeros((tq, LANES), F32)
    for t in range(SPLIT_PIECES):
        feat = jnp.where(jnp.right_shift(lane, 1) == t, slope_parts[t], feat)
        feat = jnp.where(lane == 2 * SPLIT_PIECES + t, row_parts[t], feat)
    feat = feat.astype(BF16)
    for c in range(2):
        qa_ref[c] = jnp.concatenate([q_ref[:, c * dh:(c + 1) * dh], feat], axis=1)

    def tile(start, row0, nrows, nkeys, mask_shift):
        rows = slice(row0, row0 + nrows)
        kblk = k_ref[pl.ds(start, nkeys), :]
        kfblk = kf_ref[pl.ds(start, nkeys), :]
        vblk = v_ref[pl.ds(start, nkeys), :]
        ps = []
        for c in range(2):
            ka = jnp.concatenate([kblk[:, c * dh:(c + 1) * dh], kfblk], axis=1)
            s = _dot_nt(qa_ref[c, rows, :], ka)
            if mask_shift is not None:
                r = lax.broadcasted_iota(jnp.int32, (nrows, nkeys), 0) + mask_shift
                cc = lax.broadcasted_iota(jnp.int32, (nrows, nkeys), 1)
                s = jnp.where(cc <= r, s, -jnp.inf)
            p = jnp.exp2(s)
            part = p[:, 0:LANES]
            for g in range(1, nkeys // LANES):
                part = part + p[:, g * LANES:(g + 1) * LANES]
            l_ref[c, rows, :] += part
            ps.append(p.astype(BF16))
        pv = _dot(jnp.concatenate(ps, axis=0), vblk)
        for c in range(2):
            acc_ref[c, rows, :] += pv[c * nrows:(c + 1) * nrows]

    def body(j, carry):
        tile(pl.multiple_of(j * tk, tk), 0, tq, tk, None)
        return carry

    lax.fori_loop(0, i, body, 0)
    diag = pl.multiple_of(i * tk, tk)
    half = tq // 2
    tile(diag, 0, half, half, 0)
    tile(diag, half, half, tk, half)

    inv = [jnp.broadcast_to(1.0 / jnp.sum(l_ref[c], axis=-1, keepdims=True), (tq, LANES)) for c in range(2)]
    _diff_attn_finish(acc_ref, inv[0], inv[1], sg_ref, lp_ref, ng_ref, o_ref, lam_init)


def _diff_attn(qk, vv, sig, lam_params, subln_g, lam_init, bsz, seq, tq, offset=None):
    n = qk.shape[0]
    hw = 2 * B_HEAD_DIM
    nq = seq // tq
    slopes = jnp.asarray(2.0 ** (-8.0 * np.arange(1, B_HEADS + 1) / B_HEADS) * LOG2E, dtype=F32)
    q_cols = B_QK // hw
    v_cols = A_V // hw
    g_cols = 2 * A_V // hw
    q_spec = pl.BlockSpec((tq, hw), lambda b, h, i, s: (b * nq + i, h))
    k_spec = pl.BlockSpec((seq, hw), lambda b, h, i, s: (b, q_cols + h))
    tail_specs = [pl.BlockSpec((seq, hw), lambda b, h, i, s: (b, v_cols + h)),
                  pl.BlockSpec((tq, hw), lambda b, h, i, s: (b * nq + i, g_cols + h)),
                  pl.BlockSpec((4, B_HEAD_DIM), lambda b, h, i, s: (0, 0)),
                  pl.BlockSpec((1, hw), lambda b, h, i, s: (0, 0))]
    tail_args = (vv, sig, lam_params, subln_g.reshape(1, hw))
    stats = pltpu.VMEM((2, tq, LANES), F32)
    acc = pltpu.VMEM((2, tq, hw), F32)
    if offset is None:
        kernel = functools.partial(_diff_attn_kernel, tq=tq, tk=tq, lam_init=lam_init)
        table, in_specs, args = slopes, [q_spec, k_spec] + tail_specs, (qk, qk) + tail_args
        scratch = [stats, stats, acc]
    else:
        kernel = functools.partial(_diff_attn_fixed_kernel, tq=tq, tk=tq, lam_init=lam_init)
        table = jnp.concatenate([slopes, jnp.reshape(offset, (1,)).astype(F32)])
        kf_spec = pl.BlockSpec((seq, LANES), lambda b, h, i, s: (0, 0))
        in_specs, args = [q_spec, k_spec, kf_spec] + tail_specs, (qk, qk, _key_features(seq)) + tail_args
        scratch = [stats, acc, pltpu.VMEM((2, tq, hw), BF16)]
    grid_spec = pltpu.PrefetchScalarGridSpec(
        num_scalar_prefetch=1,
        grid=(bsz, B_HEADS, nq),
        in_specs=in_specs,
        out_specs=pl.BlockSpec((tq, hw), lambda b, h, i, s: (b * nq + i, h)),
        scratch_shapes=scratch,
    )
    return pl.pallas_call(
        kernel,
        grid_spec=grid_spec,
        out_shape=jax.ShapeDtypeStruct((n, B_V), BF16),
        compiler_params=_params("parallel", "parallel", "parallel"),
        name="diff_attn" if offset is None else "diff_attn_fixed",
    )(table, *args)


def _out_proj_kernel(ya_ref, yb_ref, w_ref, x_ref, g_ref, x1_ref, h2_ref):
    y = (ya_ref[...].astype(F32) + yb_ref[...].astype(F32)).astype(BF16)
    x1 = x_ref[...] + _dot(y, w_ref[...])
    x1_ref[...] = x1
    ms = jnp.mean(x1 * x1, axis=-1, keepdims=True)
    h2_ref[...] = (x1 * lax.rsqrt(ms + EPS) * g_ref[...]).astype(h2_ref.dtype)


def _out_proj(ya, yb, w, x, g, tm):
    n, d = x.shape
    row = pl.BlockSpec((tm, d), lambda i: (i, 0))
    return pl.pallas_call(
        _out_proj_kernel,
        grid=(n // tm,),
        in_specs=[row, row, pl.BlockSpec((d, d), lambda i: (0, 0)), row,
                  pl.BlockSpec((1, d), lambda i: (0, 0))],
        out_specs=[row, row],
        out_shape=[jax.ShapeDtypeStruct((n, d), F32), jax.ShapeDtypeStruct((n, d), BF16)],
        compiler_params=_params("parallel"),
        name="out_proj",
    )(ya, yb, w, x, g.reshape(1, d))


def _ffn_up_kernel(h_ref, wg_ref, wv_ref, cwg_ref, cwv_ref, cbg_ref, cbv_ref, o_ref,
                   rg_ref, rv_ref, *, tm, kw, tiles_per_seq):
    _shift_halo(rg_ref, tm, tiles_per_seq)
    _shift_halo(rv_ref, tm, tiles_per_seq)
    h = h_ref[...]
    rg_ref[HALO:HALO + tm, :] = _dot(h, wg_ref[...])
    rv_ref[HALO:HALO + tm, :] = _dot(h, wv_ref[...])
    ug = _causal_conv(rg_ref, cwg_ref, cbg_ref, tm, kw)
    uv = _causal_conv(rv_ref, cwv_ref, cbv_ref, tm, kw)
    o_ref[...] = (ug * _sigmoid(ug) * uv).astype(o_ref.dtype)


def _ffn_up(h, w_up, conv_w, conv_b, seq, tm, tn):
    n, k = h.shape
    nj = D_FF // tn
    kw = conv_w.shape[0]
    gate = lambda r: pl.BlockSpec((r, tn), lambda j, i: (0, j))
    val = lambda r: pl.BlockSpec((r, tn), lambda j, i: (0, nj + j))
    kernel = functools.partial(_ffn_up_kernel, tm=tm, kw=kw, tiles_per_seq=seq // tm)
    return pl.pallas_call(
        kernel,
        grid=(nj, n // tm),
        in_specs=[pl.BlockSpec((tm, k), lambda j, i: (i, 0)),
                  gate(k), val(k), gate(kw), val(kw), gate(1), val(1)],
        out_specs=pl.BlockSpec((tm, tn), lambda j, i: (i, j)),
        out_shape=jax.ShapeDtypeStruct((n, D_FF), BF16),
        scratch_shapes=[pltpu.VMEM((tm + HALO, tn), F32), pltpu.VMEM((tm + HALO, tn), F32)],
        compiler_params=_params("parallel", "arbitrary"),
        name="ffn_up",
    )(h, w_up, w_up, conv_w, conv_w, conv_b, conv_b)


def _ffn_down_kernel(a_ref, w_ref, x_ref, o_ref):
    o_ref[...] = x_ref[...] + _dot(a_ref[...], w_ref[...])


def _ffn_down(act, w, x1, tm):
    n, d = x1.shape
    row = lambda c: pl.BlockSpec((tm, c), lambda i: (i, 0))
    return pl.pallas_call(
        _ffn_down_kernel,
        grid=(n // tm,),
        in_specs=[row(D_FF),
                  pl.BlockSpec((D_FF, d), lambda i: (0, 0), pipeline_mode=pl.Buffered(1)),
                  row(d)],
        out_specs=row(d),
        out_shape=jax.ShapeDtypeStruct((n, d), F32),
        compiler_params=_params("parallel"),
        name="ffn_down",
    )(act, w, x1)


def _layer(x2, bsz, seq, l, norm1_g, w_in, if_bias, qk_conv_w, qk_conv_b, mlstm_norm_g, q_norm_g,
           k_norm_g, diff_lambda, subln_g, w_out, norm2_g, w_up, ffn_conv_w, ffn_conv_b, w_down):
    n = bsz * seq
    tm = min(1024, seq)
    off = np.cumsum((0,) + SPLIT_SIZES)
    cols = lambda g: w_in[:, off[g]:off[g + 1]]
    w_aqk = cols(0).astype(BF16)
    w_plain = jnp.concatenate([cols(1), cols(6)], axis=1).astype(BF16)
    w_sig = jnp.concatenate([cols(2), cols(7), cols(8)], axis=1).astype(BF16)
    w_if = cols(3).astype(BF16)
    w_bqk = jnp.concatenate([cols(4), cols(5)], axis=1).astype(BF16)

    h, g_col, g_row = _rmsnorm_gates(x2, norm1_g, w_if, if_bias, min(512, seq))

    q_scale = jnp.concatenate([jnp.full((A_QK,), A_QK_DIM ** -0.5, F32), jnp.ones((A_QK,), F32)])
    qk_m = _proj(h, w_aqk, "conv_silu", tm, 512,
                 extra=(qk_conv_w, qk_conv_b.reshape(1, -1), q_scale.reshape(1, -1)), seq=seq)
    vv = _proj(h, w_plain, "plain", tm, 2048)
    sig = _proj(h, w_sig, "sigmoid", tm, 2048)
    reps = B_QK // B_HEAD_DIM
    qk_gain = jnp.concatenate([jnp.tile(q_norm_g * (B_HEAD_DIM ** -0.5 * LOG2E), reps), jnp.tile(k_norm_g, reps)])
    qk_d = _proj(h, w_bqk, "headnorm", tm, 2048, extra=(qk_gain.reshape(1, -1),))

    y_a = _mlstm(qk_m, vv, g_col, g_row, sig, mlstm_norm_g, bsz, seq)
    lam_init = 0.8 - 0.6 * math.exp(-0.3 * l)
    logit_bound = 1.01 * B_HEAD_DIM ** 0.5 * LOG2E * jnp.max(jnp.abs(q_norm_g * k_norm_g))
    attn = functools.partial(_diff_attn, qk_d, vv, sig, diff_lambda, subln_g, lam_init, bsz, seq)
    y_b = lax.cond(logit_bound <= FIXED_OFFSET_MAX,
                   lambda: attn(min(1024, seq), logit_bound),
                   lambda: attn(min(512, seq)))

    x1, h2 = _out_proj(y_a, y_b, w_out.astype(BF16), x2, norm2_g, min(512, seq))
    act = _ffn_up(h2, w_up.astype(BF16), ffn_conv_w, ffn_conv_b.reshape(1, -1), seq, tm, 512)
    return _ffn_down(act, w_down.astype(BF16), x1, min(512, seq))


def kernel(x, norm1_g, w_in, if_bias, qk_conv_w, qk_conv_b, mlstm_norm_g, q_norm_g, k_norm_g,
           diff_lambda, subln_g, w_out, norm2_g, w_up, ffn_conv_w, ffn_conv_b, w_down):
    bsz, seq, d = x.shape
    x2 = x.reshape(bsz * seq, d)
    for l in range(norm1_g.shape[0]):
        x2 = _layer(x2, bsz, seq, l, norm1_g[l], w_in[l], if_bias[l], qk_conv_w[l], qk_conv_b[l],
                    mlstm_norm_g[l], q_norm_g[l], k_norm_g[l], diff_lambda[l], subln_g[l], w_out[l],
                    norm2_g[l], w_up[l], ffn_conv_w[l], ffn_conv_b[l], w_down[l])
    return x2.reshape(bsz, seq, d)
```

```python
import functools
import math

import numpy as np
import jax
import jax.numpy as jnp
from jax import lax
from jax.experimental import pallas as pl
from jax.experimental.pallas import tpu as pltpu

F32 = jnp.float32
BF16 = jnp.bfloat16

D_MODEL = 2048
A_HEADS = 4
A_QK_DIM = 256
A_V_DIM = 512
A_CONV = 4
CHUNK = 256
B_HEADS = 8
B_HEAD_DIM = 128
D_FF = 5632
FFN_CONV = 3
EPS = 1e-6

A_QK = A_HEADS * A_QK_DIM
A_V = A_HEADS * A_V_DIM
B_QK = B_HEADS * 2 * B_HEAD_DIM
B_V = B_HEADS * 2 * B_HEAD_DIM
SPLIT_SIZES = (2 * A_QK, A_V, A_V, 2 * A_HEADS, B_QK, B_QK, B_V, D_MODEL, D_MODEL)

SUBLANES = 8
LANES = 128
LOG2E = math.log2(math.e)
HALO = SUBLANES
VMEM_LIMIT = 56 * 1024 * 1024


def _params(*sem):
    return pltpu.CompilerParams(dimension_semantics=sem, vmem_limit_bytes=VMEM_LIMIT)


def _dot(a, b):
    return jnp.dot(a, b, preferred_element_type=F32)


def _dot_nt(a, b):
    return lax.dot_general(a, b, (((1,), (1,)), ((), ())), preferred_element_type=F32)


def _sigmoid(x):
    return 1.0 / (1.0 + jnp.exp(-x))


def _lane_tile(x, width):
    return jnp.concatenate([x] * (width // LANES), axis=1)


def _rmsnorm_gates_kernel(x_ref, g_ref, w_ref, bc_ref, br_ref, o_ref, oc_ref, or_ref):
    x = x_ref[...]
    ms = jnp.mean(x * x, axis=-1, keepdims=True)
    h = (x * lax.rsqrt(ms + EPS) * g_ref[...]).astype(o_ref.dtype)
    o_ref[...] = h
    ng = oc_ref.shape[1]
    acc = _dot(h, w_ref[...].astype(BF16))
    oc_ref[...] = acc[:, :ng] + bc_ref[...]
    or_ref[...] = acc.T[:ng, :] + br_ref[...]


def _rmsnorm_gates(x, g, w_in, gate_col, bias, tm):
    n, d = x.shape
    ng = bias.shape[0]
    const = lambda r, c: pl.BlockSpec((r, c), lambda i: (0, 0))
    return pl.pallas_call(
        _rmsnorm_gates_kernel,
        grid=(n // tm,),
        in_specs=[pl.BlockSpec((tm, d), lambda i: (i, 0)), const(1, d),
                  pl.BlockSpec((d, LANES), lambda i: (0, gate_col // LANES)),
                  const(1, ng), const(ng, 1)],
        out_specs=[pl.BlockSpec((tm, d), lambda i: (i, 0)),
                   pl.BlockSpec((tm, ng), lambda i: (i, 0)),
                   pl.BlockSpec((ng, tm), lambda i: (0, i))],
        out_shape=[jax.ShapeDtypeStruct((n, d), BF16),
                   jax.ShapeDtypeStruct((n, ng), F32), jax.ShapeDtypeStruct((ng, n), F32)],
        compiler_params=_params("parallel"),
        name="rmsnorm_gates",
    )(x, g.reshape(1, d), w_in, bias.reshape(1, ng), bias.reshape(ng, 1))


def _proj_plain_kernel(h_ref, w_ref, o_ref):
    o_ref[...] = _dot(h_ref[...], w_ref[...]).astype(o_ref.dtype)


def _proj_sigmoid_kernel(h_ref, w_ref, o_ref):
    o_ref[...] = _sigmoid(_dot(h_ref[...], w_ref[...])).astype(o_ref.dtype)


def _proj_headnorm_kernel(h_ref, w_ref, g_ref, o_ref, *, group):
    acc = _dot(h_ref[...], w_ref[...])
    for c in range(acc.shape[1] // group):
        sl = slice(c * group, (c + 1) * group)
        blk = acc[:, sl]
        ms = jnp.mean(blk * blk, axis=-1, keepdims=True)
        o_ref[:, sl] = (blk * lax.rsqrt(ms + EPS) * g_ref[:, sl]).astype(o_ref.dtype)


def _causal_conv(raw_ref, cw_ref, cb_ref, tm, kw):
    raw = raw_ref[...]
    y = cb_ref[...] + cw_ref[kw - 1:kw, :] * raw[HALO:, :]
    for d in range(1, kw):
        y = y + cw_ref[kw - 1 - d:kw - d, :] * pltpu.roll(raw, d, axis=0)[HALO:, :]
    return y


def _shift_halo(raw_ref, tm, tiles_per_seq):
    i = pl.program_id(1)

    @pl.when(i % tiles_per_seq == 0)
    def _():
        raw_ref[0:HALO, :] = jnp.zeros((HALO, raw_ref.shape[1]), F32)

    @pl.when(i % tiles_per_seq != 0)
    def _():
        raw_ref[0:HALO, :] = raw_ref[tm:tm + HALO, :]


def _proj_conv_silu_kernel(h_ref, w_ref, cw_ref, cb_ref, ps_ref, o_ref, raw_ref, *, tm, kw, tiles_per_seq):
    _shift_halo(raw_ref, tm, tiles_per_seq)
    raw_ref[HALO:HALO + tm, :] = _dot(h_ref[...], w_ref[...])
    y = _causal_conv(raw_ref, cw_ref, cb_ref, tm, kw)
    o_ref[...] = (y * _sigmoid(y) * ps_ref[...]).astype(o_ref.dtype)


def _proj(h, w, kind, tm, tn, extra=(), seq=None):
    n, k = h.shape
    m = w.shape[1]
    grid = (m // tn, n // tm)
    h_spec = pl.BlockSpec((tm, k), lambda j, i: (i, 0))
    w_spec = pl.BlockSpec((k, tn), lambda j, i: (0, j))
    o_spec = pl.BlockSpec((tm, tn), lambda j, i: (i, j))
    row_spec = lambda r: pl.BlockSpec((r, tn), lambda j, i: (0, j))
    scratch = []
    sem = ("parallel", "parallel")
    if kind == "plain":
        body, in_specs = _proj_plain_kernel, [h_spec, w_spec]
    elif kind == "sigmoid":
        body, in_specs = _proj_sigmoid_kernel, [h_spec, w_spec]
    elif kind == "headnorm":
        body = functools.partial(_proj_headnorm_kernel, group=B_HEAD_DIM)
        in_specs = [h_spec, w_spec, row_spec(1)]
    elif kind == "conv_silu":
        kw = extra[0].shape[0]
        body = functools.partial(_proj_conv_silu_kernel, tm=tm, kw=kw, tiles_per_seq=seq // tm)
        in_specs = [h_spec, w_spec, row_spec(kw), row_spec(1), row_spec(1)]
        scratch = [pltpu.VMEM((tm + HALO, tn), F32)]
        sem = ("parallel", "arbitrary")
    else:
        raise ValueError(kind)
    return pl.pallas_call(
        body,
        grid=grid,
        in_specs=in_specs,
        out_specs=o_spec,
        out_shape=jax.ShapeDtypeStruct((n, m), BF16),
        scratch_shapes=scratch,
        compiler_params=_params(*sem),
        name="proj_" + kind,
    )(h, w, *extra)


def _log_sigmoid(x):
    return -(jnp.maximum(-x, 0.0) + jnp.log1p(jnp.exp(-jnp.abs(x))))


def _mlstm_kernel(qk_ref, v_ref, gc_ref, gr_ref, so_ref, sg_ref, ng_ref, o_ref, *state):
    L = CHUNK
    H = A_HEADS
    c_refs, m_ref = state[:H], state[H]
    ones = jnp.ones((L, LANES), BF16)

    @pl.when(pl.program_id(1) == 0)
    def _():
        for r in state:
            r[...] = jnp.zeros(r.shape, F32)

    row = lax.broadcasted_iota(jnp.int32, (L, L), 0)
    col = lax.broadcasted_iota(jnp.int32, (L, L), 1)
    lower = (col <= row)
    tri_lower = lower.astype(F32)
    tri_upper = (row <= col).astype(F32)
    exact = functools.partial(jnp.dot, preferred_element_type=F32, precision=lax.Precision.HIGHEST)

    li_c = gc_ref[:, 0:H]
    b_c = exact(tri_lower, _log_sigmoid(gc_ref[:, H:2 * H]))
    m_st = m_ref[...]
    inter = b_c + m_st
    b_last = b_c[L - 1:L, :]
    ws = b_last - b_c + li_c
    m_new = jnp.maximum(b_last + m_st, jnp.max(ws, axis=0, keepdims=True))
    decay = jnp.exp(b_last + m_st - m_new)
    ws = jnp.exp(ws - m_new)
    m_ref[...] = m_new
    d_r = exact(_log_sigmoid(gr_ref[H:2 * H, :]), tri_upper) - gr_ref[0:H, :]

    hs = []
    for hd in range(H):
        q = qk_ref[:, hd * A_QK_DIM:(hd + 1) * A_QK_DIM]
        k = qk_ref[:, A_QK + hd * A_QK_DIM:A_QK + (hd + 1) * A_QK_DIM]
        v = jnp.concatenate([v_ref[:, hd * A_V_DIM:(hd + 1) * A_V_DIM], ones], axis=1)
        c_st = c_refs[hd][...]
        inter_h = inter[:, hd:hd + 1]

        dmat = jnp.where(lower, b_c[:, hd:hd + 1] - d_r[hd:hd + 1, :], -jnp.inf)
        m_t = jnp.maximum(jnp.max(dmat, axis=-1, keepdims=True), inter_h)
        w_in = jnp.exp(dmat - m_t)
        g_inter = jnp.exp(inter_h - m_t)
        p = _dot_nt(q, k) * w_in
        num = _dot(p.astype(BF16), v) + g_inter * _dot(q, c_st.astype(BF16))
        den = num[:, A_V_DIM:]
        inv = 1.0 / jnp.maximum(jnp.abs(den), jnp.exp(-m_t))
        hs.append(num[:, :A_V_DIM] * _lane_tile(inv, A_V_DIM))

        kw = k.astype(F32) * ws[:, hd:hd + 1]
        c_refs[hd][...] = decay[:, hd:hd + 1] * c_st + _dot(kw.T.astype(BF16), v)

    for hd in range(H):
        sl = slice(hd * A_V_DIM, (hd + 1) * A_V_DIM)
        hh = hs[hd]
        ms = jnp.mean(hh * hh, axis=-1, keepdims=True)
        y = hh * lax.rsqrt(ms + EPS) * ng_ref[:, sl]
        gate = so_ref[:, sl].astype(F32) * sg_ref[:, sl].astype(F32)
        o_ref[:, sl] = (gate * y).astype(o_ref.dtype)


def _mlstm(qk, vv, g_col, g_row, sig, norm_g, bsz, seq):
    n = qk.shape[0]
    nc = seq // CHUNK
    rows = lambda b, c: b * nc + c
    return pl.pallas_call(
        _mlstm_kernel,
        grid=(bsz, nc),
        in_specs=[pl.BlockSpec((CHUNK, 2 * A_QK), lambda b, c: (rows(b, c), 0)),
                  pl.BlockSpec((CHUNK, A_V), lambda b, c: (rows(b, c), 0)),
                  pl.BlockSpec((CHUNK, 2 * A_HEADS), lambda b, c: (rows(b, c), 0)),
                  pl.BlockSpec((2 * A_HEADS, CHUNK), lambda b, c: (0, rows(b, c))),
                  pl.BlockSpec((CHUNK, A_V), lambda b, c: (rows(b, c), 0)),
                  pl.BlockSpec((CHUNK, A_V), lambda b, c: (rows(b, c), 1)),
                  pl.BlockSpec((1, A_V), lambda b, c: (0, 0))],
        out_specs=pl.BlockSpec((CHUNK, A_V), lambda b, c: (rows(b, c), 0)),
        out_shape=jax.ShapeDtypeStruct((n, A_V), BF16),
        scratch_shapes=([pltpu.VMEM((A_QK_DIM, A_V_DIM + LANES), F32)] * A_HEADS
                        + [pltpu.VMEM((1, A_HEADS), F32)]),
        compiler_params=_params("parallel", "arbitrary"),
        name="mlstm",
    )(qk, vv, g_col, g_row, sig, sig, norm_g.reshape(1, A_V))


def _diff_attn_kernel(slope_ref, q_ref, k_ref, v_ref, sg_ref, lp_ref, ng_ref, o_ref,
                      m_ref, l_ref, acc_ref, *, tq, tk, lam_init):
    dh = B_HEAD_DIM
    hw = 2 * dh
    hd = pl.program_id(1)
    i = pl.program_id(2)
    slope = slope_ref[hd]
    m_ref[...] = jnp.full(m_ref.shape, -jnp.inf, F32)
    l_ref[...] = jnp.zeros(l_ref.shape, F32)
    acc_ref[...] = jnp.zeros(acc_ref.shape, F32)
    key_pos = lax.broadcasted_iota(jnp.int32, (1, tk), 1).astype(F32)

    def step(j, masked):
        start = pl.multiple_of(j * tk, tk)
        kblk = k_ref[pl.ds(start, tk), :]
        vblk = v_ref[pl.ds(start, tk), :]
        bias = slope * (key_pos - ((i - j) * tk).astype(F32))
        ps, alphas = [], []
        for c in range(2):
            s = _dot_nt(q_ref[:, c * dh:(c + 1) * dh], kblk[:, c * dh:(c + 1) * dh]) + bias
            if masked:
                r = lax.broadcasted_iota(jnp.int32, (tq, tk), 0)
                cc = lax.broadcasted_iota(jnp.int32, (tq, tk), 1)
                s = jnp.where(cc <= r, s, -jnp.inf)
            m_old = m_ref[c]
            m_new = jnp.maximum(m_old, jnp.max(s, axis=-1, keepdims=True))
            alpha = jnp.exp2(m_old - m_new)
            p = jnp.exp2(s - _lane_tile(m_new, tk))
            l_ref[c] = alpha * l_ref[c] + jnp.sum(p, axis=-1, keepdims=True)
            m_ref[c] = m_new
            ps.append(p.astype(BF16))
            alphas.append(alpha)
        pv = _dot(jnp.concatenate(ps, axis=0), vblk)
        for c in range(2):
            acc_ref[c] = _lane_tile(alphas[c], hw) * acc_ref[c] + pv[c * tq:(c + 1) * tq]

    def body(j, carry):
        step(j, False)
        return carry

    lax.fori_loop(0, i, body, 0)
    step(i, True)

    _diff_attn_finish(acc_ref, 1.0 / l_ref[0], 1.0 / l_ref[1], sg_ref, lp_ref, ng_ref, o_ref, lam_init)


def _diff_attn_finish(acc_ref, inv0, inv1, sg_ref, lp_ref, ng_ref, o_ref, lam_init):
    hw = acc_ref.shape[-1]
    lp = lp_ref[...]
    lam = (jnp.exp(jnp.sum(lp[0:1] * lp[1:2], axis=-1, keepdims=True))
           - jnp.exp(jnp.sum(lp[2:3] * lp[3:4], axis=-1, keepdims=True)) + lam_init)
    o = acc_ref[0] * _lane_tile(inv0, hw) - acc_ref[1] * _lane_tile(lam * inv1, hw)
    ms = jnp.mean(o * o, axis=-1, keepdims=True)
    y = o * lax.rsqrt(ms + EPS) * (ng_ref[...] * (1.0 - lam_init))
    o_ref[...] = (sg_ref[...].astype(F32) * y).astype(o_ref.dtype)


FIXED_OFFSET_MAX = 48.0
SPLIT_PIECES = 3
KEY_POS_RADIX = 64


def _split_bf16(x):
    pieces = []
    for _ in range(SPLIT_PIECES):
        p = x.astype(BF16).astype(F32)
        pieces.append(p)
        x = x - p
    return pieces


def _key_features(seq):
    s = np.arange(seq)
    f = np.zeros((seq, LANES), np.float32)
    for t in range(SPLIT_PIECES):
        f[:, 2 * t] = (s // KEY_POS_RADIX) * KEY_POS_RADIX
        f[:, 2 * t + 1] = s % KEY_POS_RADIX
    f[:, 2 * SPLIT_PIECES:3 * SPLIT_PIECES] = 1.0
    return jnp.asarray(f, dtype=BF16)


def _diff_attn_fixed_kernel(tab_ref, q_ref, k_ref, kf_ref, v_ref, sg_ref, lp_ref, ng_ref, o_ref,
                            l_ref, acc_ref, qa_ref, *, tq, tk, lam_init):
    dh = B_HEAD_DIM
    hd = pl.program_id(1)
    i = pl.program_id(2)
    slope = tab_ref[hd]
    offset = tab_ref[B_HEADS]
    l_ref[...] = jnp.zeros(l_ref.shape, F32)
    acc_ref[...] = jnp.zeros(acc_ref.shape, F32)

    lane = lax.broadcasted_iota(jnp.int32, (tq, LANES), 1)
    t_pos = (lax.broadcasted_iota(jnp.int32, (tq, LANES), 0) + i * tq).astype(F32)
    slope_parts = _split_bf16(jnp.full((tq, LANES), slope, F32))
    row_parts = _split_bf16(-(slope * t_pos) - offset)
    feat = jnp.zeros((tq, LANES), F32)
    for t in range(SPLIT_PIECES):
        feat = jnp.where(jnp.right_shift(lane, 1) == t, slope_parts[t], feat)
        feat = jnp.where(lane == 2 * SPLIT_PIECES + t, row_parts[t], feat)
    feat = feat.astype(BF16)
    for c in range(2):
        qa_ref[c] = jnp.concatenate([q_ref[:, c * dh:(c + 1) * dh], feat], axis=1)

    def tile(start, row0, nrows, nkeys, mask_shift):
        rows = slice(row0, row0 + nrows)
        kblk = k_ref[pl.ds(start, nkeys), :]
        kfblk = kf_ref[pl.ds(start, nkeys), :]
        vblk = v_ref[pl.ds(start, nkeys), :]
        ps = []
        for c in range(2):
            ka = jnp.concatenate([kblk[:, c * dh:(c + 1) * dh], kfblk], axis=1)
            s = _dot_nt(qa_ref[c, rows, :], ka)
            if mask_shift is not None:
                r = lax.broadcasted_iota(jnp.int32, (nrows, nkeys), 0) + mask_shift
                cc = lax.broadcasted_iota(jnp.int32, (nrows, nkeys), 1)
                s = jnp.where(cc <= r, s, -jnp.inf)
            p = jnp.exp2(s)
            part = p[:, 0:LANES]
            for g in range(1, nkeys // LANES):
                part = part + p[:, g * LANES:(g + 1) * LANES]
            l_ref[c, rows, :] += part
            ps.append(p.astype(BF16))
        pv = _dot(jnp.concatenate(ps, axis=0), vblk)
        for c in range(2):
            acc_ref[c, rows, :] += pv[c * nrows:(c + 1) * nrows]

    def body(j, carry):
        tile(pl.multiple_of(j * tk, tk), 0, tq, tk, None)
        return carry

    lax.fori_loop(0, i, body, 0)
    diag = pl.multiple_of(i * tk, tk)
    half = tq // 2
    tile(diag, 0, half, half, 0)
    tile(diag, half, half, tk, half)

    inv = [jnp.broadcast_to(1.0 / jnp.sum(l_ref[c], axis=-1, keepdims=True), (tq, LANES)) for c in range(2)]
    _diff_attn_finish(acc_ref, inv[0], inv[1], sg_ref, lp_ref, ng_ref, o_ref, lam_init)


def _diff_attn(qk, vv, sig, lam_params, subln_g, lam_init, bsz, seq, tq, offset=None):
    n = qk.shape[0]
    hw = 2 * B_HEAD_DIM
    nq = seq // tq
    slopes = jnp.asarray(2.0 ** (-8.0 * np.arange(1, B_HEADS + 1) / B_HEADS) * LOG2E, dtype=F32)
    q_cols = B_QK // hw
    v_cols = A_V // hw
    g_cols = 2 * A_V // hw
    q_spec = pl.BlockSpec((tq, hw), lambda b, h, i, s: (b * nq + i, h))
    k_spec = pl.BlockSpec((seq, hw), lambda b, h, i, s: (b, q_cols + h))
    tail_specs = [pl.BlockSpec((seq, hw), lambda b, h, i, s: (b, v_cols + h)),
                  pl.BlockSpec((tq, hw), lambda b, h, i, s: (b * nq + i, g_cols + h)),
                  pl.BlockSpec((4, B_HEAD_DIM), lambda b, h, i, s: (0, 0)),
                  pl.BlockSpec((1, hw), lambda b, h, i, s: (0, 0))]
    tail_args = (vv, sig, lam_params, subln_g.reshape(1, hw))
    stats = pltpu.VMEM((2, tq, LANES), F32)
    acc = pltpu.VMEM((2, tq, hw), F32)
    if offset is None:
        kernel = functools.partial(_diff_attn_kernel, tq=tq, tk=tq, lam_init=lam_init)
        table, in_specs, args = slopes, [q_spec, k_spec] + tail_specs, (qk, qk) + tail_args
        scratch = [stats, stats, acc]
    else:
        kernel = functools.partial(_diff_attn_fixed_kernel, tq=tq, tk=tq, lam_init=lam_init)
        table = jnp.concatenate([slopes, jnp.reshape(offset, (1,)).astype(F32)])
        kf_spec = pl.BlockSpec((seq, LANES), lambda b, h, i, s: (0, 0))
        in_specs, args = [q_spec, k_spec, kf_spec] + tail_specs, (qk, qk, _key_features(seq)) + tail_args
        scratch = [stats, acc, pltpu.VMEM((2, tq, hw), BF16)]
    grid_spec = pltpu.PrefetchScalarGridSpec(
        num_scalar_prefetch=1,
        grid=(bsz, B_HEADS, nq),
        in_specs=in_specs,
        out_specs=pl.BlockSpec((tq, hw), lambda b, h, i, s: (b * nq + i, h)),
        scratch_shapes=scratch,
    )
    return pl.pallas_call(
        kernel,
        grid_spec=grid_spec,
        out_shape=jax.ShapeDtypeStruct((n, B_V), BF16),
        compiler_params=_params("parallel", "parallel", "parallel"),
        name="diff_attn" if offset is None else "diff_attn_fixed",
    )(table, *args)


def _out_proj_kernel(ya_ref, yb_ref, w_ref, x_ref, g_ref, x1_ref, h2_ref):
    y = (ya_ref[...].astype(F32) + yb_ref[...].astype(F32)).astype(BF16)
    x1 = x_ref[...] + _dot(y, w_ref[...])
    x1_ref[...] = x1
    ms = jnp.mean(x1 * x1, axis=-1, keepdims=True)
    h2_ref[...] = (x1 * lax.rsqrt(ms + EPS) * g_ref[...]).astype(h2_ref.dtype)


def _out_proj(ya, yb, w, x, g, tm):
    n, d = x.shape
    row = pl.BlockSpec((tm, d), lambda i: (i, 0))
    return pl.pallas_call(
        _out_proj_kernel,
        grid=(n // tm,),
        in_specs=[row, row, pl.BlockSpec((d, d), lambda i: (0, 0)), row,
                  pl.BlockSpec((1, d), lambda i: (0, 0))],
        out_specs=[row, row],
        out_shape=[jax.ShapeDtypeStruct((n, d), F32), jax.ShapeDtypeStruct((n, d), BF16)],
        compiler_params=_params("parallel"),
        name="out_proj",
    )(ya, yb, w, x, g.reshape(1, d))


def _ffn_up_kernel(h_ref, wg_ref, wv_ref, cwg_ref, cwv_ref, cbg_ref, cbv_ref, o_ref,
                   rg_ref, rv_ref, *, tm, kw, tiles_per_seq):
    _shift_halo(rg_ref, tm, tiles_per_seq)
    _shift_halo(rv_ref, tm, tiles_per_seq)
    h = h_ref[...]
    rg_ref[HALO:HALO + tm, :] = _dot(h, wg_ref[...])
    rv_ref[HALO:HALO + tm, :] = _dot(h, wv_ref[...])
    ug = _causal_conv(rg_ref, cwg_ref, cbg_ref, tm, kw)
    uv = _causal_conv(rv_ref, cwv_ref, cbv_ref, tm, kw)
    o_ref[...] = (ug * _sigmoid(ug) * uv).astype(o_ref.dtype)


def _ffn_up(h, w_up, conv_w, conv_b, seq, tm, tn):
    n, k = h.shape
    nj = D_FF // tn
    kw = conv_w.shape[0]
    gate = lambda r: pl.BlockSpec((r, tn), lambda j, i: (0, j))
    val = lambda r: pl.BlockSpec((r, tn), lambda j, i: (0, nj + j))
    kernel = functools.partial(_ffn_up_kernel, tm=tm, kw=kw, tiles_per_seq=seq // tm)
    return pl.pallas_call(
        kernel,
        grid=(nj, n // tm),
        in_specs=[pl.BlockSpec((tm, k), lambda j, i: (i, 0)),
                  gate(k), val(k), gate(kw), val(kw), gate(1), val(1)],
        out_specs=pl.BlockSpec((tm, tn), lambda j, i: (i, j)),
        out_shape=jax.ShapeDtypeStruct((n, D_FF), BF16),
        scratch_shapes=[pltpu.VMEM((tm + HALO, tn), F32), pltpu.VMEM((tm + HALO, tn), F32)],
        compiler_params=_params("parallel", "arbitrary"),
        name="ffn_up",
    )(h, w_up, w_up, conv_w, conv_w, conv_b, conv_b)


def _ffn_down_kernel(a_ref, w_ref, x_ref, o_ref):
    o_ref[...] = x_ref[...] + _dot(a_ref[...], w_ref[...])


def _ffn_down(act, w, x1, tm):
    n, d = x1.shape
    row = lambda c: pl.BlockSpec((tm, c), lambda i: (i, 0))
    return pl.pallas_call(
        _ffn_down_kernel,
        grid=(n // tm,),
        in_specs=[row(D_FF),
                  pl.BlockSpec((D_FF, d), lambda i: (0, 0), pipeline_mode=pl.Buffered(1)),
                  row(d)],
        out_specs=row(d),
        out_shape=jax.ShapeDtypeStruct((n, d), F32),
        compiler_params=_params("parallel"),
        name="ffn_down",
    )(act, w, x1)


def _layer(x2, bsz, seq, l, norm1_g, w_in, if_bias, qk_conv_w, qk_conv_b, mlstm_norm_g, q_norm_g,
           k_norm_g, diff_lambda, subln_g, w_out, norm2_g, w_up, ffn_conv_w, ffn_conv_b, w_down):
    n = bsz * seq
    tm = min(1024, seq)
    off = np.cumsum((0,) + SPLIT_SIZES)
    cols = lambda g: w_in[:, off[g]:off[g + 1]]
    w_aqk = cols(0).astype(BF16)
    w_plain = jnp.concatenate([cols(1), cols(6)], axis=1).astype(BF16)
    w_sig = jnp.concatenate([cols(2), cols(7), cols(8)], axis=1).astype(BF16)
    w_bqk = jnp.concatenate([cols(4), cols(5)], axis=1).astype(BF16)

    assert off[3] % LANES == 0 and SPLIT_SIZES[3] <= LANES
    h, g_col, g_row = _rmsnorm_gates(x2, norm1_g, w_in, int(off[3]), if_bias, min(512, seq))

    q_scale = jnp.concatenate([jnp.full((A_QK,), A_QK_DIM ** -0.5, F32), jnp.ones((A_QK,), F32)])
    qk_m = _proj(h, w_aqk, "conv_silu", tm, 512,
                 extra=(qk_conv_w, qk_conv_b.reshape(1, -1), q_scale.reshape(1, -1)), seq=seq)
    vv = _proj(h, w_plain, "plain", tm, 2048)
    sig = _proj(h, w_sig, "sigmoid", tm, 2048)
    reps = B_QK // B_HEAD_DIM
    qk_gain = jnp.concatenate([jnp.tile(q_norm_g * (B_HEAD_DIM ** -0.5 * LOG2E), reps), jnp.tile(k_norm_g, reps)])
    qk_d = _proj(h, w_bqk, "headnorm", tm, 2048, extra=(qk_gain.reshape(1, -1),))

    y_a = _mlstm(qk_m, vv, g_col, g_row, sig, mlstm_norm_g, bsz, seq)
    lam_init = 0.8 - 0.6 * math.exp(-0.3 * l)
    logit_bound = 1.01 * B_HEAD_DIM ** 0.5 * LOG2E * jnp.max(jnp.abs(q_norm_g * k_norm_g))
    attn = functools.partial(_diff_attn, qk_d, vv, sig, diff_lambda, subln_g, lam_init, bsz, seq)
    y_b = lax.cond(logit_bound <= FIXED_OFFSET_MAX,
                   lambda: attn(min(1024, seq), logit_bound),
                   lambda: attn(min(512, seq)))

    x1, h2 = _out_proj(y_a, y_b, w_out.astype(BF16), x2, norm2_g, min(512, seq))
    act = _ffn_up(h2, w_up.astype(BF16), ffn_conv_w, ffn_conv_b.reshape(1, -1), seq, tm, 512)
    return _ffn_down(act, w_down.astype(BF16), x1, min(512, seq))


def kernel(x, norm1_g, w_in, if_bias, qk_conv_w, qk_conv_b, mlstm_norm_g, q_norm_g, k_norm_g,
           diff_lambda, subln_g, w_out, norm2_g, w_up, ffn_conv_w, ffn_conv_b, w_down):
    bsz, seq, d = x.shape
    x2 = x.reshape(bsz * seq, d)
    for l in range(norm1_g.shape[0]):
        x2 = _layer(x2, bsz, seq, l, norm1_g[l], w_in[l], if_bias[l], qk_conv_w[l], qk_conv_b[l],
                    mlstm_norm_g[l], q_norm_g[l], k_norm_g[l], diff_lambda[l], subln_g[l], w_out[l],
                    norm2_g[l], w_up[l], ffn_conv_w[l], ffn_conv_b[l], w_down[l])
    return x2.reshape(bsz, seq, d)
```

```python
import functools
import math

import numpy as np
import jax
import jax.numpy as jnp
from jax import lax
from jax.experimental import pallas as pl
from jax.experimental.pallas import tpu as pltpu

F32 = jnp.float32
BF16 = jnp.bfloat16

D_MODEL = 2048
A_HEADS = 4
A_QK_DIM = 256
A_V_DIM = 512
A_CONV = 4
CHUNK = 256
B_HEADS = 8
B_HEAD_DIM = 128
D_FF = 5632
FFN_CONV = 3
EPS = 1e-6

A_QK = A_HEADS * A_QK_DIM
A_V = A_HEADS * A_V_DIM
B_QK = B_HEADS * 2 * B_HEAD_DIM
B_V = B_HEADS * 2 * B_HEAD_DIM
SPLIT_SIZES = (2 * A_QK, A_V, A_V, 2 * A_HEADS, B_QK, B_QK, B_V, D_MODEL, D_MODEL)

SUBLANES = 8
LANES = 128
LOG2E = math.log2(math.e)
HALO = SUBLANES
VMEM_LIMIT = 56 * 1024 * 1024


def _params(*sem):
    return pltpu.CompilerParams(dimension_semantics=sem, vmem_limit_bytes=VMEM_LIMIT)


def _dot(a, b):
    return jnp.dot(a, b, preferred_element_type=F32)


def _dot_nt(a, b):
    return lax.dot_general(a, b, (((1,), (1,)), ((), ())), preferred_element_type=F32)


def _sigmoid(x):
    return 1.0 / (1.0 + jnp.exp(-x))


def _lane_tile(x, width):
    return jnp.concatenate([x] * (width // LANES), axis=1)


def _rmsnorm_gates_kernel(x_ref, g_ref, w_ref, wt_ref, bc_ref, br_ref, o_ref, oc_ref, or_ref):
    x = x_ref[...]
    ms = jnp.mean(x * x, axis=-1, keepdims=True)
    h = (x * lax.rsqrt(ms + EPS) * g_ref[...]).astype(o_ref.dtype)
    o_ref[...] = h
    oc_ref[...] = _dot(h, w_ref[...]) + bc_ref[...]
    or_ref[...] = _dot_nt(wt_ref[...], h) + br_ref[...]


def _rmsnorm_gates(x, g, w_if, bias, tm):
    n, d = x.shape
    ng = w_if.shape[1]
    const = lambda r, c: pl.BlockSpec((r, c), lambda i: (0, 0))
    return pl.pallas_call(
        _rmsnorm_gates_kernel,
        grid=(n // tm,),
        in_specs=[pl.BlockSpec((tm, d), lambda i: (i, 0)), const(1, d),
                  const(d, ng), const(ng, d), const(1, ng), const(ng, 1)],
        out_specs=[pl.BlockSpec((tm, d), lambda i: (i, 0)),
                   pl.BlockSpec((tm, ng), lambda i: (i, 0)),
                   pl.BlockSpec((ng, tm), lambda i: (0, i))],
        out_shape=[jax.ShapeDtypeStruct((n, d), BF16),
                   jax.ShapeDtypeStruct((n, ng), F32), jax.ShapeDtypeStruct((ng, n), F32)],
        compiler_params=_params("parallel"),
        name="rmsnorm_gates",
    )(x, g.reshape(1, d), w_if, w_if.T, bias.reshape(1, ng), bias.reshape(ng, 1))


def _proj_plain_kernel(h_ref, w_ref, o_ref):
    o_ref[...] = _dot(h_ref[...], w_ref[...]).astype(o_ref.dtype)


def _proj_sigmoid_kernel(h_ref, w_ref, o_ref):
    o_ref[...] = _sigmoid(_dot(h_ref[...], w_ref[...])).astype(o_ref.dtype)


def _proj_headnorm_kernel(h_ref, w_ref, g_ref, o_ref, *, group):
    acc = _dot(h_ref[...], w_ref[...])
    for c in range(acc.shape[1] // group):
        sl = slice(c * group, (c + 1) * group)
        blk = acc[:, sl]
        ms = jnp.mean(blk * blk, axis=-1, keepdims=True)
        o_ref[:, sl] = (blk * lax.rsqrt(ms + EPS) * g_ref[:, sl]).astype(o_ref.dtype)


def _causal_conv(raw_ref, cw_ref, cb_ref, tm, kw):
    raw = raw_ref[...]
    y = cb_ref[...] + cw_ref[kw - 1:kw, :] * raw[HALO:, :]
    for d in range(1, kw):
        y = y + cw_ref[kw - 1 - d:kw - d, :] * pltpu.roll(raw, d, axis=0)[HALO:, :]
    return y


def _shift_halo(raw_ref, tm, tiles_per_seq):
    i = pl.program_id(1)

    @pl.when(i % tiles_per_seq == 0)
    def _():
        raw_ref[0:HALO, :] = jnp.zeros((HALO, raw_ref.shape[1]), F32)

    @pl.when(i % tiles_per_seq != 0)
    def _():
        raw_ref[0:HALO, :] = raw_ref[tm:tm + HALO, :]


def _proj_conv_silu_kernel(h_ref, w_ref, cw_ref, cb_ref, ps_ref, o_ref, raw_ref, *, tm, kw, tiles_per_seq):
    _shift_halo(raw_ref, tm, tiles_per_seq)
    raw_ref[HALO:HALO + tm, :] = _dot(h_ref[...], w_ref[...])
    y = _causal_conv(raw_ref, cw_ref, cb_ref, tm, kw)
    o_ref[...] = (y * _sigmoid(y) * ps_ref[...]).astype(o_ref.dtype)


def _proj(h, w, kind, tm, tn, extra=(), seq=None):
    n, k = h.shape
    m = w.shape[1]
    grid = (m // tn, n // tm)
    h_spec = pl.BlockSpec((tm, k), lambda j, i: (i, 0))
    w_spec = pl.BlockSpec((k, tn), lambda j, i: (0, j))
    o_spec = pl.BlockSpec((tm, tn), lambda j, i: (i, j))
    row_spec = lambda r: pl.BlockSpec((r, tn), lambda j, i: (0, j))
    scratch = []
    sem = ("parallel", "parallel")
    if kind == "plain":
        body, in_specs = _proj_plain_kernel, [h_spec, w_spec]
    elif kind == "sigmoid":
        body, in_specs = _proj_sigmoid_kernel, [h_spec, w_spec]
    elif kind == "headnorm":
        body = functools.partial(_proj_headnorm_kernel, group=B_HEAD_DIM)
        in_specs = [h_spec, w_spec, row_spec(1)]
    elif kind == "conv_silu":
        kw = extra[0].shape[0]
        body = functools.partial(_proj_conv_silu_kernel, tm=tm, kw=kw, tiles_per_seq=seq // tm)
        in_specs = [h_spec, w_spec, row_spec(kw), row_spec(1), row_spec(1)]
        scratch = [pltpu.VMEM((tm + HALO, tn), F32)]
        sem = ("parallel", "arbitrary")
    else:
        raise ValueError(kind)
    return pl.pallas_call(
        body,
        grid=grid,
        in_specs=in_specs,
        out_specs=o_spec,
        out_shape=jax.ShapeDtypeStruct((n, m), BF16),
        scratch_shapes=scratch,
        compiler_params=_params(*sem),
        name="proj_" + kind,
    )(h, w, *extra)


def _log_sigmoid(x):
    return -(jnp.maximum(-x, 0.0) + jnp.log1p(jnp.exp(-jnp.abs(x))))


def _mlstm_kernel(qk_ref, v_ref, gc_ref, gr_ref, so_ref, sg_ref, ng_ref, o_ref, *state):
    L = CHUNK
    H = A_HEADS
    c_refs, m_ref = state[:H], state[H]
    ones = jnp.ones((L, LANES), BF16)

    @pl.when(pl.program_id(1) == 0)
    def _():
        for r in state:
            r[...] = jnp.zeros(r.shape, F32)

    row = lax.broadcasted_iota(jnp.int32, (L, L), 0)
    col = lax.broadcasted_iota(jnp.int32, (L, L), 1)
    lower = (col <= row)
    tri_lower = lower.astype(F32)
    tri_upper = (row <= col).astype(F32)
    exact = functools.partial(jnp.dot, preferred_element_type=F32, precision=lax.Precision.HIGHEST)

    li_c = gc_ref[:, 0:H]
    b_c = exact(tri_lower, _log_sigmoid(gc_ref[:, H:2 * H]))
    m_st = m_ref[...]
    inter = b_c + m_st
    b_last = b_c[L - 1:L, :]
    ws = b_last - b_c + li_c
    m_new = jnp.maximum(b_last + m_st, jnp.max(ws, axis=0, keepdims=True))
    decay = jnp.exp(b_last + m_st - m_new)
    ws = jnp.exp(ws - m_new)
    m_ref[...] = m_new
    d_r = exact(_log_sigmoid(gr_ref[H:2 * H, :]), tri_upper) - gr_ref[0:H, :]

    hs = []
    for hd in range(H):
        q = qk_ref[:, hd * A_QK_DIM:(hd + 1) * A_QK_DIM]
        k = qk_ref[:, A_QK + hd * A_QK_DIM:A_QK + (hd + 1) * A_QK_DIM]
        v = jnp.concatenate([v_ref[:, hd * A_V_DIM:(hd + 1) * A_V_DIM], ones], axis=1)
        c_st = c_refs[hd][...]
        inter_h = inter[:, hd:hd + 1]

        dmat = jnp.where(lower, b_c[:, hd:hd + 1] - d_r[hd:hd + 1, :], -jnp.inf)
        m_t = jnp.maximum(jnp.max(dmat, axis=-1, keepdims=True), inter_h)
        w_in = jnp.exp(dmat - m_t)
        g_inter = jnp.exp(inter_h - m_t)
        p = _dot_nt(q, k) * w_in
        num = _dot(p.astype(BF16), v) + g_inter * _dot(q, c_st.astype(BF16))
        den = num[:, A_V_DIM:]
        inv = 1.0 / jnp.maximum(jnp.abs(den), jnp.exp(-m_t))
        hs.append(num[:, :A_V_DIM] * _lane_tile(inv, A_V_DIM))

        kw = k.astype(F32) * ws[:, hd:hd + 1]
        c_refs[hd][...] = decay[:, hd:hd + 1] * c_st + _dot(kw.T.astype(BF16), v)

    for hd in range(H):
        sl = slice(hd * A_V_DIM, (hd + 1) * A_V_DIM)
        hh = hs[hd]
        ms = jnp.mean(hh * hh, axis=-1, keepdims=True)
        y = hh * lax.rsqrt(ms + EPS) * ng_ref[:, sl]
        gate = so_ref[:, sl].astype(F32) * sg_ref[:, sl].astype(F32)
        o_ref[:, sl] = (gate * y).astype(o_ref.dtype)


def _mlstm(qk, vv, g_col, g_row, sig, norm_g, bsz, seq):
    n = qk.shape[0]
    nc = seq // CHUNK
    rows = lambda b, c: b * nc + c
    return pl.pallas_call(
        _mlstm_kernel,
        grid=(bsz, nc),
        in_specs=[pl.BlockSpec((CHUNK, 2 * A_QK), lambda b, c: (rows(b, c), 0)),
                  pl.BlockSpec((CHUNK, A_V), lambda b, c: (rows(b, c), 0)),
                  pl.BlockSpec((CHUNK, 2 * A_HEADS), lambda b, c: (rows(b, c), 0)),
                  pl.BlockSpec((2 * A_HEADS, CHUNK), lambda b, c: (0, rows(b, c))),
                  pl.BlockSpec((CHUNK, A_V), lambda b, c: (rows(b, c), 0)),
                  pl.BlockSpec((CHUNK, A_V), lambda b, c: (rows(b, c), 1)),
                  pl.BlockSpec((1, A_V), lambda b, c: (0, 0))],
        out_specs=pl.BlockSpec((CHUNK, A_V), lambda b, c: (rows(b, c), 0)),
        out_shape=jax.ShapeDtypeStruct((n, A_V), BF16),
        scratch_shapes=([pltpu.VMEM((A_QK_DIM, A_V_DIM + LANES), F32)] * A_HEADS
                        + [pltpu.VMEM((1, A_HEADS), F32)]),
        compiler_params=_params("parallel", "arbitrary"),
        name="mlstm",
    )(qk, vv, g_col, g_row, sig, sig, norm_g.reshape(1, A_V))


def _diff_attn_kernel(slope_ref, q_ref, k_ref, v_ref, sg_ref, lp_ref, ng_ref, o_ref,
                      m_ref, l_ref, acc_ref, *, tq, tk, lam_init):
    dh = B_HEAD_DIM
    hw = 2 * dh
    hd = pl.program_id(1)
    i = pl.program_id(2)
    slope = slope_ref[hd]
    m_ref[...] = jnp.full(m_ref.shape, -jnp.inf, F32)
    l_ref[...] = jnp.zeros(l_ref.shape, F32)
    acc_ref[...] = jnp.zeros(acc_ref.shape, F32)
    key_pos = lax.broadcasted_iota(jnp.int32, (1, tk), 1).astype(F32)

    def step(j, masked):
        start = pl.multiple_of(j * tk, tk)
        kblk = k_ref[pl.ds(start, tk), :]
        vblk = v_ref[pl.ds(start, tk), :]
        bias = slope * (key_pos - ((i - j) * tk).astype(F32))
        ps, alphas = [], []
        for c in range(2):
            s = _dot_nt(q_ref[:, c * dh:(c + 1) * dh], kblk[:, c * dh:(c + 1) * dh]) + bias
            if masked:
                r = lax.broadcasted_iota(jnp.int32, (tq, tk), 0)
                cc = lax.broadcasted_iota(jnp.int32, (tq, tk), 1)
                s = jnp.where(cc <= r, s, -jnp.inf)
            m_old = m_ref[c]
            m_new = jnp.maximum(m_old, jnp.max(s, axis=-1, keepdims=True))
            alpha = jnp.exp2(m_old - m_new)
            p = jnp.exp2(s - _lane_tile(m_new, tk))
            l_ref[c] = alpha * l_ref[c] + jnp.sum(p, axis=-1, keepdims=True)
            m_ref[c] = m_new
            ps.append(p.astype(BF16))
            alphas.append(alpha)
        pv = _dot(jnp.concatenate(ps, axis=0), vblk)
        for c in range(2):
            acc_ref[c] = _lane_tile(alphas[c], hw) * acc_ref[c] + pv[c * tq:(c + 1) * tq]

    def body(j, carry):
        step(j, False)
        return carry

    lax.fori_loop(0, i, body, 0)
    step(i, True)

    _diff_attn_finish(acc_ref, 1.0 / l_ref[0], 1.0 / l_ref[1], sg_ref, lp_ref, ng_ref, o_ref, lam_init)


def _diff_attn_finish(acc_ref, inv0, inv1, sg_ref, lp_ref, ng_ref, o_ref, lam_init):
    hw = acc_ref.shape[-1]
    lp = lp_ref[...]
    lam = (jnp.exp(jnp.sum(lp[0:1] * lp[1:2], axis=-1, keepdims=True))
           - jnp.exp(jnp.sum(lp[2:3] * lp[3:4], axis=-1, keepdims=True)) + lam_init)
    o = acc_ref[0] * _lane_tile(inv0, hw) - acc_ref[1] * _lane_tile(lam * inv1, hw)
    ms = jnp.mean(o * o, axis=-1, keepdims=True)
    y = o * lax.rsqrt(ms + EPS) * (ng_ref[...] * (1.0 - lam_init))
    o_ref[...] = (sg_ref[...].astype(F32) * y).astype(o_ref.dtype)


FIXED_OFFSET_MAX = 48.0
SPLIT_PIECES = 3
KEY_POS_RADIX = 64


def _split_bf16(x):
    pieces = []
    for _ in range(SPLIT_PIECES):
        p = x.astype(BF16).astype(F32)
        pieces.append(p)
        x = x - p
    return pieces


def _key_features(seq):
    s = np.arange(seq)
    f = np.zeros((seq, LANES), np.float32)
    for t in range(SPLIT_PIECES):
        f[:, 2 * t] = (s // KEY_POS_RADIX) * KEY_POS_RADIX
        f[:, 2 * t + 1] = s % KEY_POS_RADIX
    f[:, 2 * SPLIT_PIECES:3 * SPLIT_PIECES] = 1.0
    return jnp.asarray(f, dtype=BF16)


def _query_features(slopes, offset, seq):
    t = jnp.arange(seq, dtype=F32)
    row = -(slopes[:, None] * t[None, :]) - offset
    cols = []
    for piece in _split_bf16(jnp.broadcast_to(slopes[:, None], row.shape)):
        cols += [piece, piece]
    cols += _split_bf16(row)
    feat = jnp.stack(cols, axis=-1)
    return jnp.pad(feat, ((0, 0), (0, 0), (0, LANES - feat.shape[-1]))).astype(BF16)


def _diff_attn_fixed_kernel(tab_ref, q_ref, qf_ref, k_ref, kf_ref, v_ref, sg_ref, lp_ref, ng_ref, o_ref,
                            l_ref, acc_ref, qa_ref, *, tq, tk, lam_init):
    del tab_ref
    dh = B_HEAD_DIM
    i = pl.program_id(2)
    l_ref[...] = jnp.zeros(l_ref.shape, F32)
    acc_ref[...] = jnp.zeros(acc_ref.shape, F32)

    for c in range(2):
        qa_ref[c] = jnp.concatenate([q_ref[:, c * dh:(c + 1) * dh], qf_ref[0]], axis=1)

    def tile(start, row0, nrows, nkeys, mask_shift):
        rows = slice(row0, row0 + nrows)
        kblk = k_ref[pl.ds(start, nkeys), :]
        kfblk = kf_ref[pl.ds(start, nkeys), :]
        vblk = v_ref[pl.ds(start, nkeys), :]
        ps = []
        for c in range(2):
            ka = jnp.concatenate([kblk[:, c * dh:(c + 1) * dh], kfblk], axis=1)
            s = _dot_nt(qa_ref[c, rows, :], ka)
            if mask_shift is not None:
                r = lax.broadcasted_iota(jnp.int32, (nrows, nkeys), 0) + mask_shift
                cc = lax.broadcasted_iota(jnp.int32, (nrows, nkeys), 1)
                s = jnp.where(cc <= r, s, -jnp.inf)
            p = jnp.exp2(s)
            part = p[:, 0:LANES]
            for g in range(1, nkeys // LANES):
                part = part + p[:, g * LANES:(g + 1) * LANES]
            l_ref[c, rows, :] += part
            ps.append(p.astype(BF16))
        pv = _dot(jnp.concatenate(ps, axis=0), vblk)
        for c in range(2):
            acc_ref[c, rows, :] += pv[c * nrows:(c + 1) * nrows]

    def body(j, carry):
        tile(pl.multiple_of(j * tk, tk), 0, tq, tk, None)
        return carry

    lax.fori_loop(0, i, body, 0)
    diag = pl.multiple_of(i * tk, tk)
    half = tq // 2
    tile(diag, 0, half, half, 0)
    tile(diag, half, half, tk, half)

    inv = [jnp.broadcast_to(1.0 / jnp.sum(l_ref[c], axis=-1, keepdims=True), (tq, LANES)) for c in range(2)]
    _diff_attn_finish(acc_ref, inv[0], inv[1], sg_ref, lp_ref, ng_ref, o_ref, lam_init)


def _diff_attn(qk, vv, sig, lam_params, subln_g, lam_init, bsz, seq, tq, offset=None):
    n = qk.shape[0]
    hw = 2 * B_HEAD_DIM
    nq = seq // tq
    slopes = jnp.asarray(2.0 ** (-8.0 * np.arange(1, B_HEADS + 1) / B_HEADS) * LOG2E, dtype=F32)
    q_cols = B_QK // hw
    v_cols = A_V // hw
    g_cols = 2 * A_V // hw
    q_spec = pl.BlockSpec((tq, hw), lambda b, h, i, s: (b * nq + i, h))
    k_spec = pl.BlockSpec((seq, hw), lambda b, h, i, s: (b, q_cols + h))
    tail_specs = [pl.BlockSpec((seq, hw), lambda b, h, i, s: (b, v_cols + h)),
                  pl.BlockSpec((tq, hw), lambda b, h, i, s: (b * nq + i, g_cols + h)),
                  pl.BlockSpec((4, B_HEAD_DIM), lambda b, h, i, s: (0, 0)),
                  pl.BlockSpec((1, hw), lambda b, h, i, s: (0, 0))]
    tail_args = (vv, sig, lam_params, subln_g.reshape(1, hw))
    stats = pltpu.VMEM((2, tq, LANES), F32)
    acc = pltpu.VMEM((2, tq, hw), F32)
    if offset is None:
        kernel = functools.partial(_diff_attn_kernel, tq=tq, tk=tq, lam_init=lam_init)
        table, in_specs, args = slopes, [q_spec, k_spec] + tail_specs, (qk, qk) + tail_args
        scratch = [stats, stats, acc]
    else:
        kernel = functools.partial(_diff_attn_fixed_kernel, tq=tq, tk=tq, lam_init=lam_init)
        table = slopes
        qf_spec = pl.BlockSpec((1, tq, LANES), lambda b, h, i, s: (h, i, 0))
        kf_spec = pl.BlockSpec((seq, LANES), lambda b, h, i, s: (0, 0))
        in_specs = [q_spec, qf_spec, k_spec, kf_spec] + tail_specs
        args = (qk, _query_features(slopes, offset.astype(F32), seq), qk, _key_features(seq)) + tail_args
        scratch = [stats, acc, pltpu.VMEM((2, tq, hw), BF16)]
    grid_spec = pltpu.PrefetchScalarGridSpec(
        num_scalar_prefetch=1,
        grid=(bsz, B_HEADS, nq),
        in_specs=in_specs,
        out_specs=pl.BlockSpec((tq, hw), lambda b, h, i, s: (b * nq + i, h)),
        scratch_shapes=scratch,
    )
    return pl.pallas_call(
        kernel,
        grid_spec=grid_spec,
        out_shape=jax.ShapeDtypeStruct((n, B_V), BF16),
        compiler_params=_params("parallel", "parallel", "parallel"),
        name="diff_attn" if offset is None else "diff_attn_fixed",
    )(table, *args)


def _out_proj_kernel(ya_ref, yb_ref, w_ref, x_ref, g_ref, x1_ref, h2_ref):
    y = (ya_ref[...].astype(F32) + yb_ref[...].astype(F32)).astype(BF16)
    x1 = x_ref[...] + _dot(y, w_ref[...])
    x1_ref[...] = x1
    ms = jnp.mean(x1 * x1, axis=-1, keepdims=True)
    h2_ref[...] = (x1 * lax.rsqrt(ms + EPS) * g_ref[...]).astype(h2_ref.dtype)


def _out_proj(ya, yb, w, x, g, tm):
    n, d = x.shape
    row = pl.BlockSpec((tm, d), lambda i: (i, 0))
    return pl.pallas_call(
        _out_proj_kernel,
        grid=(n // tm,),
        in_specs=[row, row, pl.BlockSpec((d, d), lambda i: (0, 0)), row,
                  pl.BlockSpec((1, d), lambda i: (0, 0))],
        out_specs=[row, row],
        out_shape=[jax.ShapeDtypeStruct((n, d), F32), jax.ShapeDtypeStruct((n, d), BF16)],
        compiler_params=_params("parallel"),
        name="out_proj",
    )(ya, yb, w, x, g.reshape(1, d))


def _ffn_up_kernel(h_ref, wg_ref, wv_ref, cwg_ref, cwv_ref, cbg_ref, cbv_ref, o_ref,
                   rg_ref, rv_ref, *, tm, kw, tiles_per_seq):
    _shift_halo(rg_ref, tm, tiles_per_seq)
    _shift_halo(rv_ref, tm, tiles_per_seq)
    h = h_ref[...]
    rg_ref[HALO:HALO + tm, :] = _dot(h, wg_ref[...])
    rv_ref[HALO:HALO + tm, :] = _dot(h, wv_ref[...])
    ug = _causal_conv(rg_ref, cwg_ref, cbg_ref, tm, kw)
    uv = _causal_conv(rv_ref, cwv_ref, cbv_ref, tm, kw)
    o_ref[...] = (ug * _sigmoid(ug) * uv).astype(o_ref.dtype)


def _ffn_up(h, w_up, conv_w, conv_b, seq, tm, tn):
    n, k = h.shape
    nj = D_FF // tn
    kw = conv_w.shape[0]
    gate = lambda r: pl.BlockSpec((r, tn), lambda j, i: (0, j))
    val = lambda r: pl.BlockSpec((r, tn), lambda j, i: (0, nj + j))
    kernel = functools.partial(_ffn_up_kernel, tm=tm, kw=kw, tiles_per_seq=seq // tm)
    return pl.pallas_call(
        kernel,
        grid=(nj, n // tm),
        in_specs=[pl.BlockSpec((tm, k), lambda j, i: (i, 0)),
                  gate(k), val(k), gate(kw), val(kw), gate(1), val(1)],
        out_specs=pl.BlockSpec((tm, tn), lambda j, i: (i, j)),
        out_shape=jax.ShapeDtypeStruct((n, D_FF), BF16),
        scratch_shapes=[pltpu.VMEM((tm + HALO, tn), F32), pltpu.VMEM((tm + HALO, tn), F32)],
        compiler_params=_params("parallel", "arbitrary"),
        name="ffn_up",
    )(h, w_up, w_up, conv_w, conv_w, conv_b, conv_b)


def _ffn_down_kernel(a_ref, w_ref, x_ref, o_ref):
    o_ref[...] = x_ref[...] + _dot(a_ref[...], w_ref[...])


def _ffn_down(act, w, x1, tm):
    n, d = x1.shape
    row = lambda c: pl.BlockSpec((tm, c), lambda i: (i, 0))
    return pl.pallas_call(
        _ffn_down_kernel,
        grid=(n // tm,),
        in_specs=[row(D_FF),
                  pl.BlockSpec((D_FF, d), lambda i: (0, 0), pipeline_mode=pl.Buffered(1)),
                  row(d)],
        out_specs=row(d),
        out_shape=jax.ShapeDtypeStruct((n, d), F32),
        compiler_params=_params("parallel"),
        name="ffn_down",
    )(act, w, x1)


def _layer(x2, bsz, seq, l, norm1_g, w_in, if_bias, qk_conv_w, qk_conv_b, mlstm_norm_g, q_norm_g,
           k_norm_g, diff_lambda, subln_g, w_out, norm2_g, w_up, ffn_conv_w, ffn_conv_b, w_down):
    n = bsz * seq
    tm = min(1024, seq)
    off = np.cumsum((0,) + SPLIT_SIZES)
    cols = lambda g: w_in[:, off[g]:off[g + 1]]
    w_aqk = cols(0).astype(BF16)
    w_plain = jnp.concatenate([cols(1), cols(6)], axis=1).astype(BF16)
    w_sig = jnp.concatenate([cols(2), cols(7), cols(8)], axis=1).astype(BF16)
    w_if = cols(3).astype(BF16)
    w_bqk = jnp.concatenate([cols(4), cols(5)], axis=1).astype(BF16)

    h, g_col, g_row = _rmsnorm_gates(x2, norm1_g, w_if, if_bias, min(512, seq))

    q_scale = jnp.concatenate([jnp.full((A_QK,), A_QK_DIM ** -0.5, F32), jnp.ones((A_QK,), F32)])
    qk_m = _proj(h, w_aqk, "conv_silu", tm, 512,
                 extra=(qk_conv_w, qk_conv_b.reshape(1, -1), q_scale.reshape(1, -1)), seq=seq)
    vv = _proj(h, w_plain, "plain", tm, 2048)
    sig = _proj(h, w_sig, "sigmoid", tm, 2048)
    reps = B_QK // B_HEAD_DIM
    qk_gain = jnp.concatenate([jnp.tile(q_norm_g * (B_HEAD_DIM ** -0.5 * LOG2E), reps), jnp.tile(k_norm_g, reps)])
    qk_d = _proj(h, w_bqk, "headnorm", tm, 2048, extra=(qk_gain.reshape(1, -1),))

    y_a = _mlstm(qk_m, vv, g_col, g_row, sig, mlstm_norm_g, bsz, seq)
    lam_init = 0.8 - 0.6 * math.exp(-0.3 * l)
    logit_bound = 1.01 * B_HEAD_DIM ** 0.5 * LOG2E * jnp.max(jnp.abs(q_norm_g * k_norm_g))
    attn = functools.partial(_diff_attn, qk_d, vv, sig, diff_lambda, subln_g, lam_init, bsz, seq)
    y_b = lax.cond(logit_bound <= FIXED_OFFSET_MAX,
                   lambda: attn(min(1024, seq), logit_bound),
                   lambda: attn(min(512, seq)))

    x1, h2 = _out_proj(y_a, y_b, w_out.astype(BF16), x2, norm2_g, min(512, seq))
    act = _ffn_up(h2, w_up.astype(BF16), ffn_conv_w, ffn_conv_b.reshape(1, -1), seq, tm, 512)
    return _ffn_down(act, w_down.astype(BF16), x1, min(512, seq))


def kernel(x, norm1_g, w_in, if_bias, qk_conv_w, qk_conv_b, mlstm_norm_g, q_norm_g, k_norm_g,
           diff_lambda, subln_g, w_out, norm2_g, w_up, ffn_conv_w, ffn_conv_b, w_down):
    bsz, seq, d = x.shape
    x2 = x.reshape(bsz * seq, d)
    for l in range(norm1_g.shape[0]):
        x2 = _layer(x2, bsz, seq, l, norm1_g[l], w_in[l], if_bias[l], qk_conv_w[l], qk_conv_b[l],
                    mlstm_norm_g[l], q_norm_g[l], k_norm_g[l], diff_lambda[l], subln_g[l], w_out[l],
                    norm2_g[l], w_up[l], ffn_conv_w[l], ffn_conv_b[l], w_down[l])
    return x2.reshape(bsz, seq, d)
```
